```python
import jax, jax.numpy as jnp
from jax import lax
import numpy as np

D_MODEL = 1024
BATCH = 8
SEQ = 4096
DEPTH = 2

CHUNK = 64
RET_HEADS = 4
RET_HEAD_DIM = 128
RET_WIDTH = RET_HEADS * RET_HEAD_DIM
RWKV_HEADS = 8
RWKV_HEAD_DIM = 64
RWKV_WIDTH = RWKV_HEADS * RWKV_HEAD_DIM
DECAY_LORA = 64
AAA_LORA = 64
MV_LORA = 32
GATE_LORA = 160
D_FF = 2816
CONV_WIDTH = 3
ROPE_BASE = 10000.0
LN_EPS = 1e-5
RET_NORM_EPS = 1e-6
LNX_EPS = 64e-5
DEEPNORM_ALPHA = (2 * DEPTH) ** 0.25
DEEPNORM_BETA = (8 * DEPTH) ** -0.25
RWKV_SHIFT_WIDTH = 3 * RWKV_WIDTH + DECAY_LORA + AAA_LORA + GATE_LORA
IN_WIDTH = 4 * RET_WIDTH + RWKV_SHIFT_WIDTH + 2 * D_MODEL

kernel_name = 'hybrid_retention_rwkv7_deepnorm'


def _split(t, sizes):
    out, o = [], 0
    for s in sizes:
        out.append(t[..., o:o + s])
        o += s
    return out


def _layer_norm(x, g, b):
    xf = x.astype(jnp.float32)
    xc = xf - jnp.mean(xf, -1, keepdims=True)
    var = jnp.mean(xc * xc, -1, keepdims=True)
    return xc * lax.rsqrt(var + LN_EPS) * g.astype(jnp.float32) + b.astype(jnp.float32)


def _token_shift(z, mu):
    z_prev = jnp.pad(z[:, :-1], ((0, 0), (1, 0), (0, 0)))
    return z + mu * (z_prev - z)


def _rotary(t, pos):
    half = t.shape[-1] // 2
    inv_freq = ROPE_BASE ** (-jnp.arange(half, dtype=jnp.float32) / half)
    ang = pos[:, None] * inv_freq[None, :]
    cos = jnp.cos(ang)[None, :, None, :]
    sin = jnp.sin(ang)[None, :, None, :]
    t1, t2 = t[..., :half], t[..., half:]
    return jnp.concatenate([t1 * cos - t2 * sin, t1 * sin + t2 * cos], -1)


def _retention_core(q, k, v):
    B, S, H, Dh = q.shape
    N = S // CHUNK
    log_g = jnp.log(1.0 - 2.0 ** (-5.0 - jnp.arange(H, dtype=jnp.float32)))
    idx = jnp.arange(CHUNK, dtype=jnp.float32)
    intra_decay = jnp.exp(log_g[:, None, None] * jnp.abs(idx[:, None] - idx[None, :]))
    q_decay = jnp.exp(log_g[None, :] * (idx[:, None] + 1.0))
    k_decay = jnp.exp(log_g[None, :] * (CHUNK - 1.0 - idx[:, None]))
    chunk_decay = jnp.exp(log_g * CHUNK)
    qc = q.reshape(B, N, CHUNK, H, Dh)
    kc = k.reshape(B, N, CHUNK, H, Dh)
    vc = v.reshape(B, N, CHUNK, H, Dh)
    scores = jnp.einsum('bnchd,bnshd->bnhcs', qc, kc) * intra_decay
    intra = jnp.einsum('bnhcs,bnshd->bnchd', scores, vc)

    def step(R, inp):
        qi, ki, vi = inp
        cross = jnp.einsum('bchd,bhde->bche', qi, R) * q_decay[None, :, :, None]
        R = R * chunk_decay[None, :, None, None] + jnp.einsum(
            'bchd,bche->bhde', ki * k_decay[None, :, :, None], vi)
        return R, cross

    R0 = jnp.zeros((B, H, Dh, Dh), jnp.float32)
    xs = (jnp.moveaxis(qc, 1, 0), jnp.moveaxis(kc, 1, 0), jnp.moveaxis(vc, 1, 0))
    _, cross = lax.scan(step, R0, xs)
    cross = jnp.moveaxis(cross, 0, 1)
    return (intra + cross).reshape(B, S, H, Dh)


def _retention_branch(q, k, v, g, pos, w_o):
    B, S, _ = q.shape
    shp = (B, S, RET_HEADS, RET_HEAD_DIM)
    qh = _rotary(q.astype(jnp.float32).reshape(shp), pos)
    kh = _rotary(k.astype(jnp.float32).reshape(shp), pos) * RET_HEAD_DIM ** -0.5
    vh = v.astype(jnp.float32).reshape(shp)
    o = _retention_core(qh, kh, vh)
    o = o * lax.rsqrt(jnp.mean(o * o, -1, keepdims=True) + RET_NORM_EPS)
    o = o.reshape(B, S, RET_WIDTH) * jax.nn.silu(g.astype(jnp.float32))
    return o.astype(q.dtype) @ w_o


def _rwkv7_recurrence(r, w, k, v, a, b):
    B, S, H, N = r.shape

    def step(state, inp):
        r_t, w_t, k_t, v_t, a_t, b_t = inp
        sa = jnp.einsum('bhij,bhj->bhi', state, a_t)
        state = (state * w_t[:, :, None, :] + sa[..., None] * b_t[:, :, None, :]
                 + v_t[..., None] * k_t[:, :, None, :])
        y = jnp.einsum('bhij,bhj->bhi', state, r_t)
        return state, y

    xs = tuple(jnp.moveaxis(t, 1, 0) for t in (r, w, k, v, a, b))
    _, y = lax.scan(step, jnp.zeros((B, H, N, N), jnp.float32), xs)
    return jnp.moveaxis(y, 0, 1)


def _rwkv7_branch(z, v_first, vres, w0, w2, a0, a2, g2, k_k, k_a, r_k, lnx_g, lnx_b, w_o):
    B, S, _ = z.shape
    f32 = jnp.float32
    r, k, v, wl, al, gl = _split(z, [RWKV_WIDTH] * 3 + [DECAY_LORA, AAA_LORA, GATE_LORA])
    w_log = -jax.nn.softplus(-(w0 + jnp.tanh(wl) @ w2)) - 0.5
    decay = jnp.exp(-jnp.exp(w_log.astype(f32)))
    a = jax.nn.sigmoid(a0 + al @ a2)
    if vres is None:
        v_first = v
    else:
        vl, v0, v2 = vres
        v = v + (v_first - v) * jax.nn.sigmoid(v0 + vl @ v2)
    g = jax.nn.sigmoid(gl) @ g2

    def heads(t):
        return t.astype(f32).reshape(B, S, RWKV_HEADS, RWKV_HEAD_DIM)

    kk = heads(k * k_k)
    kk = kk / jnp.maximum(jnp.sqrt(jnp.sum(kk * kk, -1, keepdims=True)), 1e-12)
    k = k * (1.0 + (a - 1.0) * k_a)
    rh, kh, vh, ah = heads(r), heads(k), heads(v), heads(a)
    y = _rwkv7_recurrence(rh, heads(decay), kh, vh, -kk, kk * ah)
    yc = y - jnp.mean(y, -1, keepdims=True)
    yn = yc * lax.rsqrt(jnp.mean(yc * yc, -1, keepdims=True) + LNX_EPS)
    yn = yn.reshape(B, S, RWKV_WIDTH) * lnx_g + lnx_b
    bonus = jnp.sum(rh * kh * r_k, -1, keepdims=True) * vh
    y = yn + bonus.reshape(B, S, RWKV_WIDTH)
    return (y * g).astype(z.dtype) @ w_o, v_first


def _causal_dwconv(z, w, b):
    S = z.shape[1]
    zp = jnp.pad(z, ((0, 0), (CONV_WIDTH - 1, 0), (0, 0)))
    out = zp[:, 0:S] * w[0]
    for j in range(1, CONV_WIDTH):
        out = out + zp[:, j:j + S] * w[j]
    return out + b


def setup_inputs(seed: int = 0) -> dict:
    key = jax.random.key(seed)
    ks = iter(jax.random.split(key, 40))
    f32 = jnp.float32
    L, D = DEPTH, D_MODEL
    beta = DEEPNORM_BETA

    def nrm(shape, scale):
        return jax.random.normal(next(ks), shape, f32) * scale

    x = nrm((BATCH, SEQ, D), 1.0)
    col_scale = np.ones((IN_WIDTH,), np.float32)
    col_scale[2 * RET_WIDTH:3 * RET_WIDTH] = beta
    rv = 4 * RET_WIDTH + 2 * RWKV_WIDTH
    col_scale[rv:rv + RWKV_WIDTH] = beta
    w_in = nrm((L, D, IN_WIDTH), D ** -0.5) * jnp.asarray(col_scale)
    mu_shift = jax.random.uniform(next(ks), (L, RWKV_SHIFT_WIDTH), f32)
    w_vres_in = nrm((L - 1, D, MV_LORA), D ** -0.5)
    mu_vres = jax.random.uniform(next(ks), (L - 1, MV_LORA), f32)
    v0 = 1.0 + nrm((L - 1, RWKV_WIDTH), 0.1)
    v2 = nrm((L - 1, MV_LORA, RWKV_WIDTH), MV_LORA ** -0.5)
    w0 = jax.random.uniform(next(ks), (L, RWKV_WIDTH), f32, -3.0, 0.0)
    w2 = nrm((L, DECAY_LORA, RWKV_WIDTH), 0.5 * DECAY_LORA ** -0.5)
    a0 = nrm((L, RWKV_WIDTH), 0.1)
    a2 = nrm((L, AAA_LORA, RWKV_WIDTH), AAA_LORA ** -0.5)
    g2 = nrm((L, GATE_LORA, RWKV_WIDTH), GATE_LORA ** -0.5)
    k_k = 0.85 + nrm((L, RWKV_WIDTH), 0.05)
    k_a = 1.0 + nrm((L, RWKV_WIDTH), 0.05)
    r_k = nrm((L, RWKV_HEADS, RWKV_HEAD_DIM), 0.1)
    lnx_g = 1.0 + nrm((L, RWKV_WIDTH), 0.05)
    lnx_b = nrm((L, RWKV_WIDTH), 0.02)
    w_ret_o = nrm((L, RET_WIDTH, D), RET_WIDTH ** -0.5 * beta)
    w_rwkv_o = nrm((L, RWKV_WIDTH, D), RWKV_WIDTH ** -0.5 * beta)
    w_out = nrm((L, D, D), D ** -0.5 * beta)
    ln1_g = 1.0 + nrm((L, D), 0.05)
    ln1_b = nrm((L, D), 0.02)
    w_up = nrm((L, D, 2 * D_FF), D ** -0.5)
    conv_w = nrm((L, CONV_WIDTH, D_FF), CONV_WIDTH ** -0.5)
    conv_b = nrm((L, D_FF), 0.02)
    w_down = nrm((L, D_FF, D), D_FF ** -0.5 * beta)
    ln2_g = 1.0 + nrm((L, D), 0.05)
    ln2_b = nrm((L, D), 0.02)
    return {'x': x, 'w_in': w_in, 'mu_shift': mu_shift, 'w_vres_in': w_vres_in,
            'mu_vres': mu_vres, 'v0': v0, 'v2': v2, 'w0': w0, 'w2': w2, 'a0': a0,
            'a2': a2, 'g2': g2, 'k_k': k_k, 'k_a': k_a, 'r_k': r_k, 'lnx_g': lnx_g,
            'lnx_b': lnx_b, 'w_ret_o': w_ret_o, 'w_rwkv_o': w_rwkv_o, 'w_out': w_out,
            'ln1_g': ln1_g, 'ln1_b': ln1_b, 'w_up': w_up, 'conv_w': conv_w,
            'conv_b': conv_b, 'w_down': w_down, 'ln2_g': ln2_g, 'ln2_b': ln2_b}


def reference(x, w_in, mu_shift, w_vres_in, mu_vres, v0, v2, w0, w2, a0, a2, g2, k_k, k_a,
              r_k, lnx_g, lnx_b, w_ret_o, w_rwkv_o, w_out, ln1_g, ln1_b, w_up, conv_w,
              conv_b, w_down, ln2_g, ln2_b):
    dtype = x.dtype
    S = x.shape[1]
    pos = jnp.arange(S, dtype=jnp.float32)
    v_first = None
    for l in range(DEPTH):
        if l == 0:
            h = x @ w_in[0]
            vres_in = None
        else:
            h = x @ jnp.concatenate([w_in[l], w_vres_in[l - 1]], axis=1)
        q, k, v, g_ret, rw, gate_a, gate_b = _split(
            h, [RET_WIDTH] * 4 + [RWKV_SHIFT_WIDTH, D_MODEL, D_MODEL])
        if l > 0:
            vl = _token_shift(h[..., IN_WIDTH:], mu_vres[l - 1])
            vres_in = (vl, v0[l - 1], v2[l - 1])
        ret_out = _retention_branch(q, k, v, g_ret, pos, w_ret_o[l])
        rwkv_out, v_first = _rwkv7_branch(
            _token_shift(rw, mu_shift[l]), v_first, vres_in, w0[l], w2[l], a0[l], a2[l],
            g2[l], k_k[l], k_a[l], r_k[l], lnx_g[l], lnx_b[l], w_rwkv_o[l])
        mixed = jax.nn.sigmoid(gate_a) * ret_out + jax.nn.sigmoid(gate_b) * rwkv_out
        x = _layer_norm(DEEPNORM_ALPHA * x + mixed @ w_out[l], ln1_g[l], ln1_b[l]).astype(dtype)
        up = x @ w_up[l]
        gate = _causal_dwconv(up[..., :D_FF], conv_w[l], conv_b[l])
        ffn = (jax.nn.silu(gate) * up[..., D_FF:]) @ w_down[l]
        x = _layer_norm(DEEPNORM_ALPHA * x + ffn, ln2_g[l], ln2_b[l]).astype(dtype)
    return x
```

```python
import functools
import math

import jax
import jax.numpy as jnp
from jax import lax
from jax.experimental import pallas as pl
from jax.experimental.pallas import tpu as pltpu

F32 = jnp.float32
BF16 = jnp.bfloat16

D_MODEL = 1024
DEPTH = 2
CHUNK = 64
RET_HEADS = 4
RET_HEAD_DIM = 128
RET_WIDTH = RET_HEADS * RET_HEAD_DIM
RWKV_HEADS = 8
RWKV_HEAD_DIM = 64
RWKV_WIDTH = RWKV_HEADS * RWKV_HEAD_DIM
DECAY_LORA = 64
AAA_LORA = 64
MV_LORA = 32
GATE_LORA = 160
D_FF = 2816
ROPE_BASE = 10000.0
LN_EPS = 1e-5
RET_NORM_EPS = 1e-6
LNX_EPS = 64e-5
DEEPNORM_ALPHA = (2 * DEPTH) ** 0.25

RET_COLS = 4 * RET_WIDTH
RKV_COLS = 3 * RWKV_WIDTH
LORA_COLS = 512
LORA_USED = DECAY_LORA + AAA_LORA + GATE_LORA
N_IN = RET_COLS + RKV_COLS + LORA_COLS + 2 * D_MODEL
BLK = 512

VMEM_LIMIT = 56 * 1024 * 1024

NT = ((1,), (1,))
TN = ((0,), (0,))
NN = ((1,), (0,))


def _mm(a, b, dims=NN):
    return lax.dot_general(a.astype(BF16), b.astype(BF16), (dims, ((), ())),
                           preferred_element_type=F32)


def _split2(x):
    hi = x.astype(BF16)
    lo = (x - hi.astype(F32)).astype(BF16)
    return hi, lo


def _split3(x):
    hi = x.astype(BF16)
    r1 = x - hi.astype(F32)
    mid = r1.astype(BF16)
    lo = (r1 - mid.astype(F32)).astype(BF16)
    return hi, mid, lo


def _mm_exact_rhs(a_parts, b):
    acc = None
    for p in a_parts:
        t = lax.dot_general(p, b, (NN, ((), ())), preferred_element_type=F32)
        acc = t if acc is None else acc + t
    return acc


def _sigmoid(x):
    return 1.0 / (1.0 + jnp.exp(-x))


def _layer_norm(y, g, b):
    mean = jnp.mean(y, axis=-1, keepdims=True)
    yc = y - mean
    var = jnp.mean(yc * yc, axis=-1, keepdims=True)
    return yc * lax.rsqrt(var + LN_EPS) * g + b


def _params(*sem):
    return pltpu.CompilerParams(dimension_semantics=sem, vmem_limit_bytes=VMEM_LIMIT)


def _const_spec(shape):
    nd = len(shape)
    return pl.BlockSpec(shape, lambda *_: (0,) * nd, pipeline_mode=pl.Buffered(1))


def _inproj_kernel(x_ref, w_ref, o_ref):
    xb = x_ref[...].astype(BF16)
    for j in range(0, o_ref.shape[1], BLK):
        o_ref[:, j:j + BLK] = jnp.dot(xb, w_ref[:, j:j + BLK], preferred_element_type=F32)


def _inproj(x2d, w):
    m, k = x2d.shape
    n = w.shape[1]
    tm = min(256, m)
    return pl.pallas_call(
        _inproj_kernel,
        grid=(m // tm,),
        in_specs=[pl.BlockSpec((tm, k), lambda i: (i, 0)), _const_spec((k, n))],
        out_specs=pl.BlockSpec((tm, n), lambda i: (i, 0)),
        out_shape=jax.ShapeDtypeStruct((m, n), F32),
        compiler_params=_params("parallel"),
        name="inproj",
    )(x2d, w)


def _rope_kernel(cos_ref, sin_ref):
    rows = cos_ref.shape[0]
    base = pl.program_id(0) * rows
    pos = (lax.broadcasted_iota(jnp.int32, (rows, RET_HEAD_DIM), 0) + base).astype(F32)
    lane = lax.broadcasted_iota(jnp.int32, (rows, RET_HEAD_DIM), 1)
    half = RET_HEAD_DIM // 2
    fidx = jnp.where(lane < half, lane, lane - half).astype(F32)
    inv_freq = jnp.exp(fidx * (-math.log(ROPE_BASE) / half))
    ang = pos * inv_freq
    cos_ref[...] = jnp.cos(ang)
    s = jnp.sin(ang)
    sin_ref[...] = jnp.where(lane < half, -s, s)


def _rope_tables(seq):
    rows = min(256, seq)
    spec = pl.BlockSpec((rows, RET_HEAD_DIM), lambda i: (i, 0))
    shape = jax.ShapeDtypeStruct((seq, RET_HEAD_DIM), F32)
    return pl.pallas_call(
        _rope_kernel, grid=(seq // rows,), in_specs=[], out_specs=(spec, spec),
        out_shape=(shape, shape), compiler_params=_params("parallel"), name="rope_tables",
    )()


def _ret_kernel(q_ref, k_ref, v_ref, g_ref, cos_ref, sin_ref, o_ref, state_ref):
    @pl.when(pl.program_id(1) == 0)
    def _():
        state_ref[...] = jnp.zeros_like(state_ref)

    ii = lax.broadcasted_iota(jnp.int32, (CHUNK, CHUNK), 0)
    jj = lax.broadcasted_iota(jnp.int32, (CHUNK, CHUNK), 1)
    dist = jnp.abs(ii - jj).astype(F32)
    tpos = lax.broadcasted_iota(jnp.int32, (CHUNK, RET_HEAD_DIM), 0).astype(F32)
    scale = RET_HEAD_DIM ** -0.5
    for c in range(q_ref.shape[0] // CHUNK):
        rows = slice(c * CHUNK, (c + 1) * CHUNK)
        cosv = cos_ref[rows, :]
        sinv = sin_ref[rows, :]
        for h in range(RET_HEADS):
            cols = slice(h * RET_HEAD_DIM, (h + 1) * RET_HEAD_DIM)
            log_g = math.log(1.0 - 2.0 ** (-5.0 - h))
            q = q_ref[rows, cols]
            k = k_ref[rows, cols]
            v = v_ref[rows, cols]
            q = q * cosv + pltpu.roll(q, RET_HEAD_DIM // 2, 1) * sinv
            k = (k * cosv + pltpu.roll(k, RET_HEAD_DIM // 2, 1) * sinv) * scale
            scores = _mm(q, k, NT) * jnp.exp(log_g * dist)
            intra = _mm(scores, v)
            r_old = state_ref[h]
            cross = _mm(q, r_old) * jnp.exp(log_g * (tpos + 1.0))
            kd = k * jnp.exp(log_g * (CHUNK - 1.0 - tpos))
            state_ref[h] = r_old * math.exp(log_g * CHUNK) + _mm(kd, v, TN)
            o = intra + cross
            o = o * lax.rsqrt(jnp.mean(o * o, axis=-1, keepdims=True) + RET_NORM_EPS)
            g = g_ref[rows, cols]
            o_ref[rows, cols] = o * (g * _sigmoid(g))


def _retention(h3, cos_t, sin_t):
    b, s, _ = h3.shape
    t = min(256, s)

    def col(j):
        return pl.BlockSpec((None, t, BLK), lambda bi, ti, j=j: (bi, ti, j))

    tab = pl.BlockSpec((t, RET_HEAD_DIM), lambda bi, ti: (ti, 0))
    return pl.pallas_call(
        _ret_kernel,
        grid=(b, s // t),
        in_specs=[col(0), col(1), col(2), col(3), tab, tab],
        out_specs=pl.BlockSpec((None, t, RET_WIDTH), lambda bi, ti: (bi, ti, 0)),
        out_shape=jax.ShapeDtypeStruct((b, s, RET_WIDTH), F32),
        scratch_shapes=[pltpu.VMEM((RET_HEADS, RET_HEAD_DIM, RET_HEAD_DIM), F32)],
        compiler_params=_params("parallel", "arbitrary"),
        name="retention",
    )(h3, h3, h3, h3, cos_t, sin_t)


P_W0, P_A0, P_V0, P_KK, P_KA, P_RK, P_LNG, P_LNB = range(8)
SHIFT_ROWS = CHUNK + 8


def _rwkv_kernel(*refs, has_vres):
    if has_vres:
        (hr_ref, hk_ref, hv_ref, hl_ref, vfirst_ref, mu_ref, prm_ref, w2_ref, av2_ref, g2_ref,
         ones_ref, o_ref, shift_ref, state_ref, y_ref) = refs
    else:
        (hr_ref, hk_ref, hv_ref, hl_ref, mu_ref, prm_ref, w2_ref, av2_ref, g2_ref,
         ones_ref, o_ref, vout_ref, shift_ref, state_ref, y_ref) = refs

    @pl.when(pl.program_id(1) == 0)
    def _():
        shift_ref[...] = jnp.zeros_like(shift_ref)
        state_ref[...] = jnp.zeros_like(state_ref)

    cur = (hr_ref[...], hk_ref[...], hv_ref[...], hl_ref[...])
    z = []
    for i in range(4):
        cols = slice(i * BLK, (i + 1) * BLK)
        shift_ref[8:SHIFT_ROWS, cols] = cur[i]
        prev = shift_ref[7:SHIFT_ROWS - 1, cols]
        z.append(cur[i] + mu_ref[i:i + 1, :] * (prev - cur[i]))
    shift_ref[7:8, :] = shift_ref[SHIFT_ROWS - 1:SHIFT_ROWS, :]
    r, k, v, lora = z

    def prm(i):
        return prm_ref[i:i + 1, :]

    def head_sum(x):
        return _mm_exact_rhs(_split2(x), ones_ref[...])

    dw = prm(P_W0) + _mm(jnp.tanh(lora), w2_ref[...])
    softplus = jnp.maximum(-dw, 0.0) + jnp.log(1.0 + jnp.exp(-jnp.abs(dw)))
    log_decay = -jnp.exp(-softplus - 0.5)
    av = _mm(lora, av2_ref[...])
    a = _sigmoid(prm(P_A0) + av[:, :RWKV_WIDTH])
    if has_vres:
        v = v + (vfirst_ref[...] - v) * _sigmoid(prm(P_V0) + av[:, RWKV_WIDTH:])
    else:
        vout_ref[...] = v
    gate = _mm(_sigmoid(lora), g2_ref[...])
    kk = k * prm(P_KK)
    kk = kk / jnp.maximum(jnp.sqrt(head_sum(kk * kk)), 1e-12)
    k = k * (1.0 + (a - 1.0) * prm(P_KA))

    ti = lax.broadcasted_iota(jnp.int32, (CHUNK, CHUNK), 0)
    si = lax.broadcasted_iota(jnp.int32, (CHUNK, CHUNK), 1)
    incl = ti >= si
    strict = ti > si
    lower_ones = jnp.where(incl, 1.0, 0.0).astype(BF16)
    cum = None
    for part in _split3(log_decay):
        t = lax.dot_general(lower_ones, part, (NN, ((), ())), preferred_element_type=F32)
        cum = t if cum is None else cum + t
    e_in = jnp.exp(cum)
    e_neg = jnp.exp(-cum)
    a_t = -kk * jnp.exp(cum - log_decay)
    r_t = r * e_in
    b_t = kk * a * e_neg
    k_t = k * e_neg
    g_end = e_in[CHUNK - 1:CHUNK, :]
    eye = jnp.where(ti == si, 1.0, 0.0)

    for h in range(RWKV_HEADS):
        cols = slice(h * RWKV_HEAD_DIM, (h + 1) * RWKV_HEAD_DIM)
        ah, rh, bh, kh, vh = a_t[:, cols], r_t[:, cols], b_t[:, cols], k_t[:, cols], v[:, cols]
        a_ab = jnp.where(strict, _mm(ah, bh, NT), 0.0)
        a_ak = jnp.where(strict, _mm(ah, kh, NT), 0.0)
        a_rb = jnp.where(incl, _mm(rh, bh, NT), 0.0)
        a_rk = jnp.where(incl, _mm(rh, kh, NT), 0.0)
        tinv = eye + a_ab
        pw = a_ab
        for _ in range(5):
            pw = _mm(pw, pw)
            tinv = tinv + _mm(tinv, pw)
        s_old = state_ref[h]
        u = _mm(tinv, _mm(ah, s_old, NT) + _mm(a_ak, vh))
        y_ref[:, cols] = _mm(rh, s_old, NT) + _mm(a_rb, u) + _mm(a_rk, vh)
        state_ref[h] = g_end[:, cols] * (s_old + _mm(u, bh, TN) + _mm(vh, kh, TN))

    y = y_ref[...]
    inv_n = 1.0 / RWKV_HEAD_DIM
    yc = y - head_sum(y) * inv_n
    yn = yc * lax.rsqrt(head_sum(yc * yc) * inv_n + LNX_EPS)
    yn = yn * prm(P_LNG) + prm(P_LNB)
    bonus = head_sum(r * k * prm(P_RK)) * v
    o_ref[...] = (yn + bonus) * gate


def _rwkv(h3, v_first, mu, prm, w2f, av2f, g2f, ones_blk):
    b, s, _ = h3.shape
    has_vres = v_first is not None
    first_col = RET_COLS // BLK

    def col(j):
        return pl.BlockSpec((None, CHUNK, BLK), lambda bi, ci, j=j: (bi, ci, j))

    tok = pl.BlockSpec((None, CHUNK, RWKV_WIDTH), lambda bi, ci: (bi, ci, 0))
    tok_shape = jax.ShapeDtypeStruct((b, s, RWKV_WIDTH), F32)
    in_specs = [col(first_col), col(first_col + 1), col(first_col + 2), col(first_col + 3)]
    args = [h3, h3, h3, h3]
    if has_vres:
        in_specs.append(tok)
        args.append(v_first)
    consts = (mu, prm, w2f, av2f, g2f, ones_blk)
    in_specs += [_const_spec(c.shape) for c in consts]
    args += list(consts)
    out = pl.pallas_call(
        functools.partial(_rwkv_kernel, has_vres=has_vres),
        grid=(b, s // CHUNK),
        in_specs=in_specs,
        out_specs=tok if has_vres else (tok, tok),
        out_shape=tok_shape if has_vres else (tok_shape, tok_shape),
        scratch_shapes=[
            pltpu.VMEM((SHIFT_ROWS, 4 * BLK), F32),
            pltpu.VMEM((RWKV_HEADS, RWKV_HEAD_DIM, RWKV_HEAD_DIM), F32),
            pltpu.VMEM((CHUNK, RWKV_WIDTH), F32),
        ],
        compiler_params=_params("parallel", "arbitrary"),
        name="rwkv7",
    )(*args)
    return (out, v_first) if has_vres else out


def _merge_kernel(x_ref, ret_ref, rwkv_ref, ga_ref, gb_ref, wret_ref, wrwkv_ref, wout_ref,
                  lng_ref, lnb_ref, o_ref):
    ret_out = _mm(ret_ref[...], wret_ref[...])
    rwkv_out = _mm(rwkv_ref[...], wrwkv_ref[...])
    mixed = _sigmoid(ga_ref[...]) * ret_out + _sigmoid(gb_ref[...]) * rwkv_out
    y = DEEPNORM_ALPHA * x_ref[...] + _mm(mixed, wout_ref[...])
    o_ref[...] = _layer_norm(y, lng_ref[...], lnb_ref[...])


def _merge(x2d, ret2d, rwkv2d, h2d, w_ret, w_rwkv, w_out, ln_g, ln_b):
    m = x2d.shape[0]
    tm = min(256, m)
    gate_blk = (RET_COLS + RKV_COLS + LORA_COLS) // D_MODEL

    def rows(width, j=0):
        return pl.BlockSpec((tm, width), lambda i, j=j: (i, j))

    consts = (w_ret, w_rwkv, w_out, ln_g, ln_b)
    return pl.pallas_call(
        _merge_kernel,
        grid=(m // tm,),
        in_specs=[rows(D_MODEL), rows(RET_WIDTH), rows(RWKV_WIDTH), rows(D_MODEL, gate_blk),
                  rows(D_MODEL, gate_blk + 1)] + [_const_spec(c.shape) for c in consts],
        out_specs=rows(D_MODEL),
        out_shape=jax.ShapeDtypeStruct((m, D_MODEL), F32),
        compiler_params=_params("parallel"),
        name="merge_ln",
    )(x2d, ret2d, rwkv2d, h2d, h2d, *consts)


FF_CHUNK = D_FF // 2
CONV_PAD = 8


def _ffn_kernel(x_ref, wg_ref, wv_ref, wd_ref, cw_ref, cb_ref, lng_ref, lnb_ref, o_ref,
                gate_ref):
    @pl.when(pl.program_id(1) == 0)
    def _():
        gate_ref[...] = jnp.zeros_like(gate_ref)

    t = x_ref.shape[0]
    x = x_ref[...]
    xb = x.astype(BF16)
    acc = DEEPNORM_ALPHA * x
    for c in range(D_FF // FF_CHUNK):
        cols = slice(c * FF_CHUNK, (c + 1) * FF_CHUNK)
        gate_ref[c, CONV_PAD:, :] = jnp.dot(xb, wg_ref[:, cols], preferred_element_type=F32)
        conv = (gate_ref[c, CONV_PAD - 2:CONV_PAD - 2 + t, :] * cw_ref[0:1, cols]
                + gate_ref[c, CONV_PAD - 1:CONV_PAD - 1 + t, :] * cw_ref[1:2, cols]
                + gate_ref[c, CONV_PAD:, :] * cw_ref[2:3, cols] + cb_ref[:, cols])
        gate_ref[c, CONV_PAD - 2:CONV_PAD, :] = gate_ref[c, CONV_PAD + t - 2:CONV_PAD + t, :]
        val = jnp.dot(xb, wv_ref[:, cols], preferred_element_type=F32)
        act = conv * _sigmoid(conv) * val
        acc = acc + _mm(act, wd_ref[cols, :])
    o_ref[...] = _layer_norm(acc, lng_ref[...], lnb_ref[...])


def _ffn(x3, w_gate, w_val, w_down, conv_w, conv_b, ln_g, ln_b):
    b, s, _ = x3.shape
    t = min(256, s)
    tok = pl.BlockSpec((None, t, D_MODEL), lambda bi, ti: (bi, ti, 0))
    consts = (w_gate, w_val, w_down, conv_w, conv_b, ln_g, ln_b)
    return pl.pallas_call(
        _ffn_kernel,
        grid=(b, s // t),
        in_specs=[tok] + [_const_spec(c.shape) for c in consts],
        out_specs=tok,
        out_shape=jax.ShapeDtypeStruct((b, s, D_MODEL), F32),
        scratch_shapes=[pltpu.VMEM((D_FF // FF_CHUNK, CONV_PAD + t, FF_CHUNK), F32)],
        compiler_params=_params("parallel", "arbitrary"),
        name="conv_mlp_ln",
    )(x3, *consts)


def _row(v):
    return v.reshape(1, -1).astype(F32)


def _pad_rows(w, start, total=LORA_COLS):
    return jnp.pad(w, ((start, total - start - w.shape[0]), (0, 0))).astype(BF16)


def kernel(x, w_in, mu_shift, w_vres_in, mu_vres, v0, v2, w0, w2, a0, a2, g2, k_k, k_a, r_k,
           lnx_g, lnx_b, w_ret_o, w_rwkv_o, w_out, ln1_g, ln1_b, w_up, conv_w, conv_b, w_down,
           ln2_g, ln2_b):
    b, s, d = x.shape
    assert d == D_MODEL and s % CHUNK == 0
    m = b * s
    cos_t, sin_t = _rope_tables(s)
    head_id = jnp.arange(RWKV_WIDTH) // RWKV_HEAD_DIM
    ones_blk = (head_id[:, None] == head_id[None, :]).astype(BF16)
    split = RET_COLS + RKV_COLS + LORA_USED
    lora_pad = LORA_COLS - LORA_USED - MV_LORA
    off_a, off_g, off_v = DECAY_LORA, DECAY_LORA + AAA_LORA, LORA_USED

    v_first = None
    for l in range(DEPTH):
        vres_w = w_vres_in[l - 1] if l > 0 else jnp.zeros((D_MODEL, MV_LORA), F32)
        vres_mu = mu_vres[l - 1] if l > 0 else jnp.zeros((MV_LORA,), F32)
        w_cat = jnp.concatenate(
            [w_in[l][:, :split], vres_w, jnp.zeros((D_MODEL, lora_pad), F32), w_in[l][:, split:]],
            axis=1).astype(BF16)
        mu_lora = jnp.concatenate(
            [mu_shift[l][RKV_COLS:], vres_mu, jnp.zeros((lora_pad,), F32)])
        mu = jnp.concatenate([mu_shift[l][:RKV_COLS].reshape(3, RWKV_WIDTH), mu_lora[None]], 0)
        v0_l = v0[l - 1] if l > 0 else jnp.zeros((RWKV_WIDTH,), F32)
        prm = jnp.stack([w0[l], a0[l], v0_l, k_k[l], k_a[l], r_k[l].reshape(-1), lnx_g[l],
                         lnx_b[l]]).astype(F32)
        w2f = _pad_rows(w2[l], 0)
        v2_l = v2[l - 1] if l > 0 else jnp.zeros((MV_LORA, RWKV_WIDTH), F32)
        av2f = jnp.concatenate([_pad_rows(a2[l], off_a), _pad_rows(v2_l, off_v)], axis=1)
        g2f = _pad_rows(g2[l], off_g)

        x2d = x.reshape(m, D_MODEL)
        h2d = _inproj(x2d, w_cat)
        h3 = h2d.reshape(b, s, N_IN)
        ret = _retention(h3, cos_t, sin_t)
        rwkv, v_first = _rwkv(h3, v_first, mu, prm, w2f, av2f, g2f, ones_blk)
        x1 = _merge(x2d, ret.reshape(m, RET_WIDTH), rwkv.reshape(m, RWKV_WIDTH), h2d,
                    w_ret_o[l].astype(BF16), w_rwkv_o[l].astype(BF16), w_out[l].astype(BF16),
                    _row(ln1_g[l]), _row(ln1_b[l]))
        x = _ffn(x1.reshape(b, s, D_MODEL), w_up[l][:, :D_FF].astype(BF16),
                 w_up[l][:, D_FF:].astype(BF16), w_down[l].astype(BF16), conv_w[l].astype(F32),
                 _row(conv_b[l]), _row(ln2_g[l]), _row(ln2_b[l]))
    return x
```

```python
import functools
import math

import jax
import jax.numpy as jnp
from jax import lax
from jax.experimental import pallas as pl
from jax.experimental.pallas import tpu as pltpu

F32 = jnp.float32
BF16 = jnp.bfloat16

D_MODEL = 1024
DEPTH = 2
CHUNK = 64
RET_HEADS = 4
RET_HEAD_DIM = 128
RET_WIDTH = RET_HEADS * RET_HEAD_DIM
RWKV_HEADS = 8
RWKV_HEAD_DIM = 64
RWKV_WIDTH = RWKV_HEADS * RWKV_HEAD_DIM
DECAY_LORA = 64
AAA_LORA = 64
MV_LORA = 32
GATE_LORA = 160
D_FF = 2816
ROPE_BASE = 10000.0
LN_EPS = 1e-5
RET_NORM_EPS = 1e-6
LNX_EPS = 64e-5
DEEPNORM_ALPHA = (2 * DEPTH) ** 0.25

RET_COLS = 4 * RET_WIDTH
RKV_COLS = 3 * RWKV_WIDTH
LORA_COLS = 512
LORA_USED = DECAY_LORA + AAA_LORA + GATE_LORA
N_IN = RET_COLS + RKV_COLS + LORA_COLS + 2 * D_MODEL
BLK = 512

VMEM_LIMIT = 56 * 1024 * 1024

NT = ((1,), (1,))
TN = ((0,), (0,))
NN = ((1,), (0,))


def _mm(a, b, dims=NN):
    return lax.dot_general(a.astype(BF16), b.astype(BF16), (dims, ((), ())),
                           preferred_element_type=F32)


def _split2(x):
    hi = x.astype(BF16)
    lo = (x - hi.astype(F32)).astype(BF16)
    return hi, lo


def _split3(x):
    hi = x.astype(BF16)
    r1 = x - hi.astype(F32)
    mid = r1.astype(BF16)
    lo = (r1 - mid.astype(F32)).astype(BF16)
    return hi, mid, lo


def _mm_exact_rhs(a_parts, b):
    acc = None
    for p in a_parts:
        t = lax.dot_general(p, b, (NN, ((), ())), preferred_element_type=F32)
        acc = t if acc is None else acc + t
    return acc


def _sigmoid(x):
    return 1.0 / (1.0 + jnp.exp(-x))


def _layer_norm(y, g, b):
    mean = jnp.mean(y, axis=-1, keepdims=True)
    yc = y - mean
    var = jnp.mean(yc * yc, axis=-1, keepdims=True)
    return yc * lax.rsqrt(var + LN_EPS) * g + b


def _params(*sem):
    return pltpu.CompilerParams(dimension_semantics=sem, vmem_limit_bytes=VMEM_LIMIT)


def _const_spec(shape):
    nd = len(shape)
    return pl.BlockSpec(shape, lambda *_: (0,) * nd, pipeline_mode=pl.Buffered(1))


def _inproj_kernel(x_ref, w_ref, o_ref):
    xb = x_ref[...].astype(BF16)
    for j in range(0, o_ref.shape[1], BLK):
        o_ref[:, j:j + BLK] = jnp.dot(xb, w_ref[:, j:j + BLK], preferred_element_type=F32)


def _inproj(x2d, w):
    m, k = x2d.shape
    n = w.shape[1]
    tm = min(256, m)
    return pl.pallas_call(
        _inproj_kernel,
        grid=(m // tm,),
        in_specs=[pl.BlockSpec((tm, k), lambda i: (i, 0)), _const_spec((k, n))],
        out_specs=pl.BlockSpec((tm, n), lambda i: (i, 0)),
        out_shape=jax.ShapeDtypeStruct((m, n), F32),
        compiler_params=_params("parallel"),
        name="inproj",
    )(x2d, w)


def _rope_kernel(cos_ref, sin_ref):
    rows = cos_ref.shape[0]
    base = pl.program_id(0) * rows
    pos = (lax.broadcasted_iota(jnp.int32, (rows, RET_HEAD_DIM), 0) + base).astype(F32)
    lane = lax.broadcasted_iota(jnp.int32, (rows, RET_HEAD_DIM), 1)
    half = RET_HEAD_DIM // 2
    fidx = jnp.where(lane < half, lane, lane - half).astype(F32)
    inv_freq = jnp.exp(fidx * (-math.log(ROPE_BASE) / half))
    ang = pos * inv_freq
    cos_ref[...] = jnp.cos(ang)
    s = jnp.sin(ang)
    sin_ref[...] = jnp.where(lane < half, -s, s)


def _rope_tables(seq):
    rows = min(256, seq)
    spec = pl.BlockSpec((rows, RET_HEAD_DIM), lambda i: (i, 0))
    shape = jax.ShapeDtypeStruct((seq, RET_HEAD_DIM), F32)
    return pl.pallas_call(
        _rope_kernel, grid=(seq // rows,), in_specs=[], out_specs=(spec, spec),
        out_shape=(shape, shape), compiler_params=_params("parallel"), name="rope_tables",
    )()


def _ret_kernel(q_ref, k_ref, v_ref, g_ref, cos_ref, sin_ref, o_ref, state_ref):
    @pl.when(pl.program_id(1) == 0)
    def _():
        state_ref[...] = jnp.zeros_like(state_ref)

    ii = lax.broadcasted_iota(jnp.int32, (CHUNK, CHUNK), 0)
    jj = lax.broadcasted_iota(jnp.int32, (CHUNK, CHUNK), 1)
    dist = jnp.abs(ii - jj).astype(F32)
    tpos = lax.broadcasted_iota(jnp.int32, (CHUNK, RET_HEAD_DIM), 0).astype(F32)
    scale = RET_HEAD_DIM ** -0.5
    for c in range(q_ref.shape[0] // CHUNK):
        rows = slice(c * CHUNK, (c + 1) * CHUNK)
        cosv = cos_ref[rows, :]
        sinv = sin_ref[rows, :]
        for h in range(RET_HEADS):
            cols = slice(h * RET_HEAD_DIM, (h + 1) * RET_HEAD_DIM)
            log_g = math.log(1.0 - 2.0 ** (-5.0 - h))
            q = q_ref[rows, cols]
            k = k_ref[rows, cols]
            v = v_ref[rows, cols]
            q = q * cosv + pltpu.roll(q, RET_HEAD_DIM // 2, 1) * sinv
            k = (k * cosv + pltpu.roll(k, RET_HEAD_DIM // 2, 1) * sinv) * scale
            scores = _mm(q, k, NT) * jnp.exp(log_g * dist)
            intra = _mm(scores, v)
            r_old = state_ref[h]
            cross = _mm(q, r_old) * jnp.exp(log_g * (tpos + 1.0))
            kd = k * jnp.exp(log_g * (CHUNK - 1.0 - tpos))
            state_ref[h] = r_old * math.exp(log_g * CHUNK) + _mm(kd, v, TN)
            o = intra + cross
            o = o * lax.rsqrt(jnp.mean(o * o, axis=-1, keepdims=True) + RET_NORM_EPS)
            g = g_ref[rows, cols]
            o_ref[rows, cols] = o * (g * _sigmoid(g))


def _retention(h3, cos_t, sin_t):
    b, s, _ = h3.shape
    t = min(256, s)

    def col(j):
        return pl.BlockSpec((None, t, BLK), lambda bi, ti, j=j: (bi, ti, j))

    tab = pl.BlockSpec((t, RET_HEAD_DIM), lambda bi, ti: (ti, 0))
    return pl.pallas_call(
        _ret_kernel,
        grid=(b, s // t),
        in_specs=[col(0), col(1), col(2), col(3), tab, tab],
        out_specs=pl.BlockSpec((None, t, RET_WIDTH), lambda bi, ti: (bi, ti, 0)),
        out_shape=jax.ShapeDtypeStruct((b, s, RET_WIDTH), F32),
        scratch_shapes=[pltpu.VMEM((RET_HEADS, RET_HEAD_DIM, RET_HEAD_DIM), F32)],
        compiler_params=_params("parallel", "arbitrary"),
        name="retention",
    )(h3, h3, h3, h3, cos_t, sin_t)


P_W0, P_A0, P_V0, P_KK, P_KA, P_RK, P_LNG, P_LNB = range(8)
RWKV_STEP_CHUNKS = 4
RWKV_STEP = RWKV_STEP_CHUNKS * CHUNK
SHIFT_PAD = 8
PAIR = 2 * RWKV_HEAD_DIM
N_PAIRS = RWKV_HEADS // 2


def _rwkv_kernel(*refs, has_vres):
    if has_vres:
        (hr_ref, hk_ref, hv_ref, hl_ref, vfirst_ref, mu_ref, prm_ref, w2_ref, av2_ref, g2_ref,
         ones_ref, o_ref, shift_ref, state_ref) = refs
    else:
        (hr_ref, hk_ref, hv_ref, hl_ref, mu_ref, prm_ref, w2_ref, av2_ref, g2_ref,
         ones_ref, o_ref, vout_ref, shift_ref, state_ref) = refs
    rows_t = hr_ref.shape[0]
    n_chunks = rows_t // CHUNK

    @pl.when(pl.program_id(1) == 0)
    def _():
        shift_ref[...] = jnp.zeros_like(shift_ref)
        state_ref[...] = jnp.zeros_like(state_ref)

    cur = (hr_ref[...], hk_ref[...], hv_ref[...], hl_ref[...])
    z = []
    for i in range(4):
        cols = slice(i * BLK, (i + 1) * BLK)
        shift_ref[SHIFT_PAD:, cols] = cur[i]
        prev = shift_ref[SHIFT_PAD - 1:SHIFT_PAD - 1 + rows_t, cols]
        z.append(cur[i] + mu_ref[i:i + 1, :] * (prev - cur[i]))
    shift_ref[SHIFT_PAD - 1:SHIFT_PAD, :] = shift_ref[SHIFT_PAD + rows_t - 1:SHIFT_PAD + rows_t, :]
    r, k, v, lora = z

    def prm(i):
        return prm_ref[i:i + 1, :]

    def head_sum(x):
        return _mm_exact_rhs(_split2(x), ones_ref[...])

    dw = prm(P_W0) + _mm(jnp.tanh(lora), w2_ref[...])
    softplus = jnp.maximum(-dw, 0.0) + jnp.log(1.0 + jnp.exp(-jnp.abs(dw)))
    log_decay = -jnp.exp(-softplus - 0.5)
    av = _mm(lora, av2_ref[...])
    a = _sigmoid(prm(P_A0) + av[:, :RWKV_WIDTH])
    if has_vres:
        v = v + (vfirst_ref[...] - v) * _sigmoid(prm(P_V0) + av[:, RWKV_WIDTH:])
    else:
        vout_ref[...] = v
    gate = _mm(_sigmoid(lora), g2_ref[...])
    kk = k * prm(P_KK)
    kk = kk / jnp.maximum(jnp.sqrt(head_sum(kk * kk)), 1e-12)
    k = k * (1.0 + (a - 1.0) * prm(P_KA))

    ti = lax.broadcasted_iota(jnp.int32, (rows_t, rows_t), 0)
    si = lax.broadcasted_iota(jnp.int32, (rows_t, rows_t), 1)
    lower = jnp.where(ti >= si, jnp.where((ti // CHUNK) == (si // CHUNK), 1.0, 0.0), 0.0).astype(BF16)
    cum = None
    for part in _split3(log_decay):
        t = lax.dot_general(lower, part, (NN, ((), ())), preferred_element_type=F32)
        cum = t if cum is None else cum + t
    e_in = jnp.exp(cum)
    e_neg = jnp.exp(-cum)
    a_t = -kk * jnp.exp(cum - log_decay)
    r_t = r * e_in
    b_t = kk * a * e_neg
    k_t = k * e_neg

    row = lax.broadcasted_iota(jnp.int32, (PAIR, PAIR), 0)
    lane = lax.broadcasted_iota(jnp.int32, (PAIR, PAIR), 1)
    hd = RWKV_HEAD_DIM
    row_hi = jnp.where(row < hd, 0, 1)
    lane_hi = jnp.where(lane < hd, 0, 1)
    tri_mask = (row - hd * row_hi + row_hi) > (lane - hd * lane_hi)
    bd_mask = row_hi == lane_hi
    lane_lo = lax.broadcasted_iota(jnp.int32, (CHUNK, PAIR), 1) < hd
    e0 = lax.broadcasted_iota(jnp.int32, (CHUNK, CHUNK), 0)
    e1 = lax.broadcasted_iota(jnp.int32, (CHUNK, CHUNK), 1)
    eye = jnp.where(e0 == e1, 1.0, 0.0)
    zeros = jnp.zeros((CHUNK, PAIR), F32)

    chunks = range(n_chunks)
    pairs = range(N_PAIRS)
    units = [(c, p, e) for c in chunks for p in pairs for e in range(2)]

    def blk(x, c, p):
        return x[c * CHUNK:(c + 1) * CHUNK, p * PAIR:(p + 1) * PAIR]

    bkt, gcol = {}, {}
    for c in chunks:
        rows = slice(c * CHUNK, (c + 1) * CHUNK)
        bk_t = jnp.concatenate([b_t[rows, :], k_t[rows, :]], axis=0).T
        g_t = jnp.broadcast_to(e_in[(c + 1) * CHUNK - 1:(c + 1) * CHUNK, :], (PAIR, RWKV_WIDTH)).T
        for p in pairs:
            bkt[c, p] = bk_t[p * PAIR:(p + 1) * PAIR, :]
            gcol[c, p] = g_t[p * PAIR:(p + 1) * PAIR, :]
    aa = {}
    for (c, p, e) in units:
        m = lane_lo if e == 0 else jnp.logical_not(lane_lo)
        lhs = jnp.concatenate([jnp.where(m, blk(a_t, c, p), 0.0), jnp.where(m, blk(r_t, c, p), 0.0)], axis=0)
        aa[c, p, e] = jnp.where(tri_mask, _mm(lhs, bkt[c, p]), 0.0)
    pw = {u: aa[u][:CHUNK, :CHUNK] for u in units}
    tinv = {u: eye + pw[u] for u in units}
    for _ in range(5):
        for u in units:
            pw[u] = _mm(pw[u], pw[u])
        for u in units:
            tinv[u] = tinv[u] + _mm(tinv[u], pw[u])
    zv = {(c, p): jnp.concatenate([zeros, blk(v, c, p)], axis=0) for c in chunks for p in pairs}
    akv = {(c, p, e): _mm(aa[c, p, e][:CHUNK, :], zv[c, p]) for (c, p, e) in units}
    tw = {(c, p, e): _mm(tinv[c, p, e], jnp.concatenate([blk(a_t, c, p), akv[c, p, e]], axis=1))
          for (c, p, e) in units}
    wr, ut, kv = {}, {}, {}
    for c in chunks:
        for p in pairs:
            w = jnp.where(lane_lo, tw[c, p, 0][:, :PAIR], tw[c, p, 1][:, :PAIR])
            ut[c, p] = jnp.where(lane_lo, tw[c, p, 0][:, PAIR:], tw[c, p, 1][:, PAIR:])
            wr[c, p] = jnp.concatenate([w, blk(r_t, c, p)], axis=0)
            kv[c, p] = jnp.where(bd_mask, _mm(bkt[c, p], zv[c, p]), 0.0)

    hs = [state_ref[p] for p in pairs]
    y_rows = []
    for c in chunks:
        x = [_mm(wr[c, p], hs[p]) for p in pairs]
        u = [x[p][:CHUNK, :] + ut[c, p] for p in pairs]
        hs = [gcol[c, p] * (hs[p] + kv[c, p]
                            + jnp.where(bd_mask, _mm(bkt[c, p][:, :CHUNK], u[p]), 0.0)) for p in pairs]
        y_p = []
        for p in pairs:
            uv = jnp.concatenate([u[p], blk(v, c, p)], axis=0)
            y0 = _mm(aa[c, p, 0][CHUNK:, :], uv)
            y1 = _mm(aa[c, p, 1][CHUNK:, :], uv)
            y_p.append(x[p][CHUNK:, :] + jnp.where(lane_lo, y0, y1))
        y_rows.append(jnp.concatenate(y_p, axis=1))
    for p in pairs:
        state_ref[p] = hs[p]
    y = jnp.concatenate(y_rows, axis=0)

    inv_n = 1.0 / RWKV_HEAD_DIM
    yc = y - head_sum(y) * inv_n
    yn = yc * lax.rsqrt(head_sum(yc * yc) * inv_n + LNX_EPS)
    yn = yn * prm(P_LNG) + prm(P_LNB)
    bonus = head_sum(r * k * prm(P_RK)) * v
    o_ref[...] = (yn + bonus) * gate


def _rwkv(h3, v_first, mu, prm, w2f, av2f, g2f, ones_blk):
    b, s, _ = h3.shape
    has_vres = v_first is not None
    first_col = RET_COLS // BLK
    t = min(RWKV_STEP, s)

    def col(j):
        return pl.BlockSpec((None, t, BLK), lambda bi, ci, j=j: (bi, ci, j))

    tok = pl.BlockSpec((None, t, RWKV_WIDTH), lambda bi, ci: (bi, ci, 0))
    tok_shape = jax.ShapeDtypeStruct((b, s, RWKV_WIDTH), F32)
    in_specs = [col(first_col), col(first_col + 1), col(first_col + 2), col(first_col + 3)]
    args = [h3, h3, h3, h3]
    if has_vres:
        in_specs.append(tok)
        args.append(v_first)
    consts = (mu, prm, w2f, av2f, g2f, ones_blk)
    in_specs += [_const_spec(c.shape) for c in consts]
    args += list(consts)
    out = pl.pallas_call(
        functools.partial(_rwkv_kernel, has_vres=has_vres),
        grid=(b, s // t),
        in_specs=in_specs,
        out_specs=tok if has_vres else (tok, tok),
        out_shape=tok_shape if has_vres else (tok_shape, tok_shape),
        scratch_shapes=[
            pltpu.VMEM((SHIFT_PAD + t, 4 * BLK), F32),
            pltpu.VMEM((N_PAIRS, PAIR, PAIR), F32),
        ],
        compiler_params=_params("parallel", "arbitrary"),
        name="rwkv7",
    )(*args)
    return (out, v_first) if has_vres else out


def _merge_kernel(x_ref, ret_ref, rwkv_ref, ga_ref, gb_ref, wret_ref, wrwkv_ref, wout_ref,
                  lng_ref, lnb_ref, o_ref):
    ret_out = _mm(ret_ref[...], wret_ref[...])
    rwkv_out = _mm(rwkv_ref[...], wrwkv_ref[...])
    mixed = _sigmoid(ga_ref[...]) * ret_out + _sigmoid(gb_ref[...]) * rwkv_out
    y = DEEPNORM_ALPHA * x_ref[...] + _mm(mixed, wout_ref[...])
    o_ref[...] = _layer_norm(y, lng_ref[...], lnb_ref[...])


def _merge(x2d, ret2d, rwkv2d, h2d, w_ret, w_rwkv, w_out, ln_g, ln_b):
    m = x2d.shape[0]
    tm = min(256, m)
    gate_blk = (RET_COLS + RKV_COLS + LORA_COLS) // D_MODEL

    def rows(width, j=0):
        return pl.BlockSpec((tm, width), lambda i, j=j: (i, j))

    consts = (w_ret, w_rwkv, w_out, ln_g, ln_b)
    return pl.pallas_call(
        _merge_kernel,
        grid=(m // tm,),
        in_specs=[rows(D_MODEL), rows(RET_WIDTH), rows(RWKV_WIDTH), rows(D_MODEL, gate_blk),
                  rows(D_MODEL, gate_blk + 1)] + [_const_spec(c.shape) for c in consts],
        out_specs=rows(D_MODEL),
        out_shape=jax.ShapeDtypeStruct((m, D_MODEL), F32),
        compiler_params=_params("parallel"),
        name="merge_ln",
    )(x2d, ret2d, rwkv2d, h2d, h2d, *consts)


FF_CHUNK = D_FF // 2
CONV_PAD = 8


def _ffn_kernel(x_ref, wg_ref, wv_ref, wd_ref, cw_ref, cb_ref, lng_ref, lnb_ref, o_ref,
                gate_ref):
    @pl.when(pl.program_id(1) == 0)
    def _():
        gate_ref[...] = jnp.zeros_like(gate_ref)

    t = x_ref.shape[0]
    x = x_ref[...]
    xb = x.astype(BF16)
    acc = DEEPNORM_ALPHA * x
    for c in range(D_FF // FF_CHUNK):
        cols = slice(c * FF_CHUNK, (c + 1) * FF_CHUNK)
        gate_ref[c, CONV_PAD:, :] = jnp.dot(xb, wg_ref[:, cols], preferred_element_type=F32)
        conv = (gate_ref[c, CONV_PAD - 2:CONV_PAD - 2 + t, :] * cw_ref[0:1, cols]
                + gate_ref[c, CONV_PAD - 1:CONV_PAD - 1 + t, :] * cw_ref[1:2, cols]
                + gate_ref[c, CONV_PAD:, :] * cw_ref[2:3, cols] + cb_ref[:, cols])
        gate_ref[c, CONV_PAD - 2:CONV_PAD, :] = gate_ref[c, CONV_PAD + t - 2:CONV_PAD + t, :]
        val = jnp.dot(xb, wv_ref[:, cols], preferred_element_type=F32)
        act = conv * _sigmoid(conv) * val
        acc = acc + _mm(act, wd_ref[cols, :])
    o_ref[...] = _layer_norm(acc, lng_ref[...], lnb_ref[...])


def _ffn(x3, w_gate, w_val, w_down, conv_w, conv_b, ln_g, ln_b):
    b, s, _ = x3.shape
    t = min(256, s)
    tok = pl.BlockSpec((None, t, D_MODEL), lambda bi, ti: (bi, ti, 0))
    consts = (w_gate, w_val, w_down, conv_w, conv_b, ln_g, ln_b)
    return pl.pallas_call(
        _ffn_kernel,
        grid=(b, s // t),
        in_specs=[tok] + [_const_spec(c.shape) for c in consts],
        out_specs=tok,
        out_shape=jax.ShapeDtypeStruct((b, s, D_MODEL), F32),
        scratch_shapes=[pltpu.VMEM((D_FF // FF_CHUNK, CONV_PAD + t, FF_CHUNK), F32)],
        compiler_params=_params("parallel", "arbitrary"),
        name="conv_mlp_ln",
    )(x3, *consts)


def _row(v):
    return v.reshape(1, -1).astype(F32)


def _pad_rows(w, start, total=LORA_COLS):
    return jnp.pad(w, ((start, total - start - w.shape[0]), (0, 0))).astype(BF16)


def kernel(x, w_in, mu_shift, w_vres_in, mu_vres, v0, v2, w0, w2, a0, a2, g2, k_k, k_a, r_k,
           lnx_g, lnx_b, w_ret_o, w_rwkv_o, w_out, ln1_g, ln1_b, w_up, conv_w, conv_b, w_down,
           ln2_g, ln2_b):
    b, s, d = x.shape
    assert d == D_MODEL and s % CHUNK == 0
    m = b * s
    cos_t, sin_t = _rope_tables(s)
    head_id = jnp.arange(RWKV_WIDTH) // RWKV_HEAD_DIM
    ones_blk = (head_id[:, None] == head_id[None, :]).astype(BF16)
    split = RET_COLS + RKV_COLS + LORA_USED
    lora_pad = LORA_COLS - LORA_USED - MV_LORA
    off_a, off_g, off_v = DECAY_LORA, DECAY_LORA + AAA_LORA, LORA_USED

    v_first = None
    for l in range(DEPTH):
        vres_w = w_vres_in[l - 1] if l > 0 else jnp.zeros((D_MODEL, MV_LORA), F32)
        vres_mu = mu_vres[l - 1] if l > 0 else jnp.zeros((MV_LORA,), F32)
        w_cat = jnp.concatenate(
            [w_in[l][:, :split], vres_w, jnp.zeros((D_MODEL, lora_pad), F32), w_in[l][:, split:]],
            axis=1).astype(BF16)
        mu_lora = jnp.concatenate(
            [mu_shift[l][RKV_COLS:], vres_mu, jnp.zeros((lora_pad,), F32)])
        mu = jnp.concatenate([mu_shift[l][:RKV_COLS].reshape(3, RWKV_WIDTH), mu_lora[None]], 0)
        v0_l = v0[l - 1] if l > 0 else jnp.zeros((RWKV_WIDTH,), F32)
        prm = jnp.stack([w0[l], a0[l], v0_l, k_k[l], k_a[l], r_k[l].reshape(-1), lnx_g[l],
                         lnx_b[l]]).astype(F32)
        w2f = _pad_rows(w2[l], 0)
        v2_l = v2[l - 1] if l > 0 else jnp.zeros((MV_LORA, RWKV_WIDTH), F32)
        av2f = jnp.concatenate([_pad_rows(a2[l], off_a), _pad_rows(v2_l, off_v)], axis=1)
        g2f = _pad_rows(g2[l], off_g)

        x2d = x.reshape(m, D_MODEL)
        h2d = _inproj(x2d, w_cat)
        h3 = h2d.reshape(b, s, N_IN)
        ret = _retention(h3, cos_t, sin_t)
        rwkv, v_first = _rwkv(h3, v_first, mu, prm, w2f, av2f, g2f, ones_blk)
        x1 = _merge(x2d, ret.reshape(m, RET_WIDTH), rwkv.reshape(m, RWKV_WIDTH), h2d,
                    w_ret_o[l].astype(BF16), w_rwkv_o[l].astype(BF16), w_out[l].astype(BF16),
                    _row(ln1_g[l]), _row(ln1_b[l]))
        x = _ffn(x1.reshape(b, s, D_MODEL), w_up[l][:, :D_FF].astype(BF16),
                 w_up[l][:, D_FF:].astype(BF16), w_down[l].astype(BF16), conv_w[l].astype(F32),
                 _row(conv_b[l]), _row(ln2_g[l]), _row(ln2_b[l]))
    return x
```

```python
import functools
import math

import jax
import jax.numpy as jnp
from jax import lax
from jax.experimental import pallas as pl
from jax.experimental.pallas import tpu as pltpu

F32 = jnp.float32
BF16 = jnp.bfloat16

D_MODEL = 1024
DEPTH = 2
CHUNK = 64
RET_HEADS = 4
RET_HEAD_DIM = 128
RET_WIDTH = RET_HEADS * RET_HEAD_DIM
RWKV_HEADS = 8
RWKV_HEAD_DIM = 64
RWKV_WIDTH = RWKV_HEADS * RWKV_HEAD_DIM
DECAY_LORA = 64
AAA_LORA = 64
MV_LORA = 32
GATE_LORA = 160
D_FF = 2816
ROPE_BASE = 10000.0
LN_EPS = 1e-5
RET_NORM_EPS = 1e-6
LNX_EPS = 64e-5
DEEPNORM_ALPHA = (2 * DEPTH) ** 0.25

RET_COLS = 4 * RET_WIDTH
RKV_COLS = 3 * RWKV_WIDTH
LORA_COLS = 512
LORA_USED = DECAY_LORA + AAA_LORA + GATE_LORA
N_IN = RET_COLS + RKV_COLS + LORA_COLS + 2 * D_MODEL
BLK = 512

VMEM_LIMIT = 56 * 1024 * 1024
ROW_TILE = 512

NT = ((1,), (1,))
TN = ((0,), (0,))
NN = ((1,), (0,))


def _mm(a, b, dims=NN):
    return lax.dot_general(a.astype(BF16), b.astype(BF16), (dims, ((), ())),
                           preferred_element_type=F32)


def _split2(x):
    hi = x.astype(BF16)
    lo = (x - hi.astype(F32)).astype(BF16)
    return hi, lo


def _split3(x):
    hi = x.astype(BF16)
    r1 = x - hi.astype(F32)
    mid = r1.astype(BF16)
    lo = (r1 - mid.astype(F32)).astype(BF16)
    return hi, mid, lo


def _mm_exact_rhs(a_parts, b):
    acc = None
    for p in a_parts:
        t = lax.dot_general(p, b, (NN, ((), ())), preferred_element_type=F32)
        acc = t if acc is None else acc + t
    return acc


def _sigmoid(x):
    return 1.0 / (1.0 + jnp.exp(-x))


def _layer_norm(y, g, b):
    mean = jnp.mean(y, axis=-1, keepdims=True)
    yc = y - mean
    var = jnp.mean(yc * yc, axis=-1, keepdims=True)
    return yc * lax.rsqrt(var + LN_EPS) * g + b


def _params(*sem):
    return pltpu.CompilerParams(dimension_semantics=sem, vmem_limit_bytes=VMEM_LIMIT)


def _const_spec(shape):
    nd = len(shape)
    return pl.BlockSpec(shape, lambda *_: (0,) * nd, pipeline_mode=pl.Buffered(1))


def _inproj_kernel(x_ref, w_ref, o_ref):
    xb = x_ref[...].astype(BF16)
    for j in range(0, o_ref.shape[1], BLK):
        o_ref[:, j:j + BLK] = jnp.dot(xb, w_ref[:, j:j + BLK],
                                      preferred_element_type=F32).astype(o_ref.dtype)


def _inproj(x2d, w):
    m, k = x2d.shape
    n = w.shape[1]
    tm = min(ROW_TILE, m)
    return pl.pallas_call(
        _inproj_kernel,
        grid=(m // tm,),
        in_specs=[pl.BlockSpec((tm, k), lambda i: (i, 0)), _const_spec((k, n))],
        out_specs=pl.BlockSpec((tm, n), lambda i: (i, 0)),
        out_shape=jax.ShapeDtypeStruct((m, n), BF16),
        compiler_params=_params("parallel"),
        name="inproj",
    )(x2d, w)


def _rope_kernel(cos_ref, sin_ref):
    rows = cos_ref.shape[0]
    base = pl.program_id(0) * rows
    pos = (lax.broadcasted_iota(jnp.int32, (rows, RET_HEAD_DIM), 0) + base).astype(F32)
    lane = lax.broadcasted_iota(jnp.int32, (rows, RET_HEAD_DIM), 1)
    half = RET_HEAD_DIM // 2
    fidx = jnp.where(lane < half, lane, lane - half).astype(F32)
    inv_freq = jnp.exp(fidx * (-math.log(ROPE_BASE) / half))
    ang = pos * inv_freq
    cos_ref[...] = jnp.cos(ang)
    s = jnp.sin(ang)
    sin_ref[...] = jnp.where(lane < half, -s, s)


def _rope_tables(seq):
    rows = min(256, seq)
    spec = pl.BlockSpec((rows, RET_HEAD_DIM), lambda i: (i, 0))
    shape = jax.ShapeDtypeStruct((seq, RET_HEAD_DIM), F32)
    return pl.pallas_call(
        _rope_kernel, grid=(seq // rows,), in_specs=[], out_specs=(spec, spec),
        out_shape=(shape, shape), compiler_params=_params("parallel"), name="rope_tables",
    )()


def _ret_kernel(q_ref, k_ref, v_ref, g_ref, cos_ref, sin_ref, o_ref, state_ref):
    @pl.when(pl.program_id(1) == 0)
    def _():
        state_ref[...] = jnp.zeros_like(state_ref)

    ii = lax.broadcasted_iota(jnp.int32, (CHUNK, CHUNK), 0)
    jj = lax.broadcasted_iota(jnp.int32, (CHUNK, CHUNK), 1)
    dist = jnp.abs(ii - jj).astype(F32)
    tpos = lax.broadcasted_iota(jnp.int32, (CHUNK, RET_HEAD_DIM), 0).astype(F32)
    scale = RET_HEAD_DIM ** -0.5
    chunks = range(q_ref.shape[0] // CHUNK)
    heads = range(RET_HEADS)
    units = [(c, h) for c in chunks for h in heads]
    log_g = [math.log(1.0 - 2.0 ** (-5.0 - h)) for h in heads]
    intra_decay = [jnp.exp(log_g[h] * dist) for h in heads]
    q_decay = [jnp.exp(log_g[h] * (tpos + 1.0)) for h in heads]
    k_decay = [jnp.exp(log_g[h] * (CHUNK - 1.0 - tpos)) for h in heads]

    q, k, v = {}, {}, {}
    for (c, h) in units:
        rows = slice(c * CHUNK, (c + 1) * CHUNK)
        cols = slice(h * RET_HEAD_DIM, (h + 1) * RET_HEAD_DIM)
        cosv = cos_ref[rows, :]
        sinv = sin_ref[rows, :]
        qq = q_ref[rows, cols].astype(F32)
        kk = k_ref[rows, cols].astype(F32)
        q[c, h] = qq * cosv + pltpu.roll(qq, RET_HEAD_DIM // 2, 1) * sinv
        k[c, h] = (kk * cosv + pltpu.roll(kk, RET_HEAD_DIM // 2, 1) * sinv) * scale
        v[c, h] = v_ref[rows, cols]
    scores = {u: _mm(q[u], k[u], NT) * intra_decay[u[1]] for u in units}
    kv = {u: _mm(k[u] * k_decay[u[1]], v[u], TN) for u in units}
    intra = {u: _mm(scores[u], v[u]) for u in units}
    r_in = {}
    for h in heads:
        state = state_ref[h]
        for c in chunks:
            r_in[c, h] = state
            state = state * math.exp(log_g[h] * CHUNK) + kv[c, h]
        state_ref[h] = state
    cross = {u: _mm(q[u], r_in[u]) * q_decay[u[1]] for u in units}
    for (c, h) in units:
        rows = slice(c * CHUNK, (c + 1) * CHUNK)
        cols = slice(h * RET_HEAD_DIM, (h + 1) * RET_HEAD_DIM)
        o = intra[c, h] + cross[c, h]
        o = o * lax.rsqrt(jnp.mean(o * o, axis=-1, keepdims=True) + RET_NORM_EPS)
        g = g_ref[rows, cols].astype(F32)
        o_ref[rows, cols] = (o * (g * _sigmoid(g))).astype(o_ref.dtype)


def _retention(h3, cos_t, sin_t):
    b, s, _ = h3.shape
    t = min(256, s)

    def col(j):
        return pl.BlockSpec((None, t, BLK), lambda bi, ti, j=j: (bi, ti, j))

    tab = pl.BlockSpec((t, RET_HEAD_DIM), lambda bi, ti: (ti, 0))
    return pl.pallas_call(
        _ret_kernel,
        grid=(b, s // t),
        in_specs=[col(0), col(1), col(2), col(3), tab, tab],
        out_specs=pl.BlockSpec((None, t, RET_WIDTH), lambda bi, ti: (bi, ti, 0)),
        out_shape=jax.ShapeDtypeStruct((b, s, RET_WIDTH), BF16),
        scratch_shapes=[pltpu.VMEM((RET_HEADS, RET_HEAD_DIM, RET_HEAD_DIM), F32)],
        compiler_params=_params("parallel", "arbitrary"),
        name="retention",
    )(h3, h3, h3, h3, cos_t, sin_t)


P_W0, P_A0, P_V0, P_KK, P_KA, P_RK, P_LNG, P_LNB = range(8)
RWKV_STEP_CHUNKS = 4
RWKV_STEP = RWKV_STEP_CHUNKS * CHUNK
SHIFT_PAD = 8
PAIR = 2 * RWKV_HEAD_DIM
N_PAIRS = RWKV_HEADS // 2


def _rwkv_kernel(*refs, has_vres):
    if has_vres:
        (hr_ref, hk_ref, hv_ref, hl_ref, vfirst_ref, mu_ref, prm_ref, w2_ref, av2_ref, g2_ref,
         ones_ref, o_ref, shift_ref, state_ref) = refs
    else:
        (hr_ref, hk_ref, hv_ref, hl_ref, mu_ref, prm_ref, w2_ref, av2_ref, g2_ref,
         ones_ref, o_ref, vout_ref, shift_ref, state_ref) = refs
    rows_t = hr_ref.shape[0]
    n_chunks = rows_t // CHUNK

    @pl.when(pl.program_id(1) == 0)
    def _():
        shift_ref[...] = jnp.zeros_like(shift_ref)
        state_ref[...] = jnp.zeros_like(state_ref)

    cur = tuple(ref[...].astype(F32) for ref in (hr_ref, hk_ref, hv_ref, hl_ref))
    z = []
    for i in range(4):
        cols = slice(i * BLK, (i + 1) * BLK)
        shift_ref[SHIFT_PAD:, cols] = cur[i]
        prev = shift_ref[SHIFT_PAD - 1:SHIFT_PAD - 1 + rows_t, cols]
        z.append(cur[i] + mu_ref[i:i + 1, :] * (prev - cur[i]))
    shift_ref[SHIFT_PAD - 1:SHIFT_PAD, :] = shift_ref[SHIFT_PAD + rows_t - 1:SHIFT_PAD + rows_t, :]
    r, k, v, lora = z

    def prm(i):
        return prm_ref[i:i + 1, :]

    def head_sum(x):
        return _mm_exact_rhs(_split2(x), ones_ref[...])

    dw = prm(P_W0) + _mm(jnp.tanh(lora), w2_ref[...])
    softplus = jnp.maximum(-dw, 0.0) + jnp.log(1.0 + jnp.exp(-jnp.abs(dw)))
    log_decay = -jnp.exp(-softplus - 0.5)
    av = _mm(lora, av2_ref[...])
    a = _sigmoid(prm(P_A0) + av[:, :RWKV_WIDTH])
    if has_vres:
        v = v + (vfirst_ref[...] - v) * _sigmoid(prm(P_V0) + av[:, RWKV_WIDTH:])
    else:
        vout_ref[...] = v
    gate = _mm(_sigmoid(lora), g2_ref[...])
    kk = k * prm(P_KK)
    kk = kk / jnp.maximum(jnp.sqrt(head_sum(kk * kk)), 1e-12)
    k = k * (1.0 + (a - 1.0) * prm(P_KA))

    ti = lax.broadcasted_iota(jnp.int32, (rows_t, rows_t), 0)
    si = lax.broadcasted_iota(jnp.int32, (rows_t, rows_t), 1)
    lower = jnp.where(ti >= si, jnp.where((ti // CHUNK) == (si // CHUNK), 1.0, 0.0), 0.0).astype(BF16)
    cum = None
    for part in _split3(log_decay):
        t = lax.dot_general(lower, part, (NN, ((), ())), preferred_element_type=F32)
        cum = t if cum is None else cum + t
    e_in = jnp.exp(cum)
    e_neg = jnp.exp(-cum)
    a_t = -kk * jnp.exp(cum - log_decay)
    r_t = r * e_in
    b_t = kk * a * e_neg
    k_t = k * e_neg

    row = lax.broadcasted_iota(jnp.int32, (PAIR, PAIR), 0)
    lane = lax.broadcasted_iota(jnp.int32, (PAIR, PAIR), 1)
    hd = RWKV_HEAD_DIM
    row_hi = jnp.where(row < hd, 0, 1)
    lane_hi = jnp.where(lane < hd, 0, 1)
    tri_mask = (row - hd * row_hi + row_hi) > (lane - hd * lane_hi)
    bd_mask = row_hi == lane_hi
    lane_lo = lax.broadcasted_iota(jnp.int32, (CHUNK, PAIR), 1) < hd
    e0 = lax.broadcasted_iota(jnp.int32, (CHUNK, CHUNK), 0)
    e1 = lax.broadcasted_iota(jnp.int32, (CHUNK, CHUNK), 1)
    eye = jnp.where(e0 == e1, 1.0, 0.0)
    zeros = jnp.zeros((CHUNK, PAIR), F32)

    chunks = range(n_chunks)
    pairs = range(N_PAIRS)
    units = [(c, p, e) for c in chunks for p in pairs for e in range(2)]

    def blk(x, c, p):
        return x[c * CHUNK:(c + 1) * CHUNK, p * PAIR:(p + 1) * PAIR]

    bkt, gcol = {}, {}
    for c in chunks:
        rows = slice(c * CHUNK, (c + 1) * CHUNK)
        bk_t = jnp.concatenate([b_t[rows, :], k_t[rows, :]], axis=0).T
        g_t = jnp.broadcast_to(e_in[(c + 1) * CHUNK - 1:(c + 1) * CHUNK, :], (PAIR, RWKV_WIDTH)).T
        for p in pairs:
            bkt[c, p] = bk_t[p * PAIR:(p + 1) * PAIR, :]
            gcol[c, p] = g_t[p * PAIR:(p + 1) * PAIR, :]
    aa = {}
    for (c, p, e) in units:
        m = lane_lo if e == 0 else jnp.logical_not(lane_lo)
        lhs = jnp.concatenate([jnp.where(m, blk(a_t, c, p), 0.0), jnp.where(m, blk(r_t, c, p), 0.0)], axis=0)
        aa[c, p, e] = jnp.where(tri_mask, _mm(lhs, bkt[c, p]), 0.0)
    pw = {u: aa[u][:CHUNK, :CHUNK] for u in units}
    tinv = {u: eye + pw[u] for u in units}
    for _ in range(5):
        for u in units:
            pw[u] = _mm(pw[u], pw[u])
        for u in units:
            tinv[u] = tinv[u] + _mm(tinv[u], pw[u])
    zv = {(c, p): jnp.concatenate([zeros, blk(v, c, p)], axis=0) for c in chunks for p in pairs}
    akv = {(c, p, e): _mm(aa[c, p, e][:CHUNK, :], zv[c, p]) for (c, p, e) in units}
    tw = {(c, p, e): _mm(tinv[c, p, e], jnp.concatenate([blk(a_t, c, p), akv[c, p, e]], axis=1))
          for (c, p, e) in units}
    wr, ut, kv = {}, {}, {}
    for c in chunks:
        for p in pairs:
            w = jnp.where(lane_lo, tw[c, p, 0][:, :PAIR], tw[c, p, 1][:, :PAIR])
            ut[c, p] = jnp.where(lane_lo, tw[c, p, 0][:, PAIR:], tw[c, p, 1][:, PAIR:])
            wr[c, p] = jnp.concatenate([w, blk(r_t, c, p)], axis=0)
            kv[c, p] = jnp.where(bd_mask, _mm(bkt[c, p], zv[c, p]), 0.0)

    hs = [state_ref[p] for p in pairs]
    y_rows = []
    for c in chunks:
        x = [_mm(wr[c, p], hs[p]) for p in pairs]
        u = [x[p][:CHUNK, :] + ut[c, p] for p in pairs]
        hs = [gcol[c, p] * (hs[p] + kv[c, p]
                            + jnp.where(bd_mask, _mm(bkt[c, p][:, :CHUNK], u[p]), 0.0)) for p in pairs]
        y_p = []
        for p in pairs:
            uv = jnp.concatenate([u[p], blk(v, c, p)], axis=0)
            y0 = _mm(aa[c, p, 0][CHUNK:, :], uv)
            y1 = _mm(aa[c, p, 1][CHUNK:, :], uv)
            y_p.append(x[p][CHUNK:, :] + jnp.where(lane_lo, y0, y1))
        y_rows.append(jnp.concatenate(y_p, axis=1))
    for p in pairs:
        state_ref[p] = hs[p]
    y = jnp.concatenate(y_rows, axis=0)

    inv_n = 1.0 / RWKV_HEAD_DIM
    yc = y - head_sum(y) * inv_n
    yn = yc * lax.rsqrt(head_sum(yc * yc) * inv_n + LNX_EPS)
    yn = yn * prm(P_LNG) + prm(P_LNB)
    bonus = head_sum(r * k * prm(P_RK)) * v
    o_ref[...] = ((yn + bonus) * gate).astype(o_ref.dtype)


def _rwkv(h3, v_first, mu, prm, w2f, av2f, g2f, ones_blk):
    b, s, _ = h3.shape
    has_vres = v_first is not None
    first_col = RET_COLS // BLK
    t = min(RWKV_STEP, s)

    def col(j):
        return pl.BlockSpec((None, t, BLK), lambda bi, ci, j=j: (bi, ci, j))

    tok = pl.BlockSpec((None, t, RWKV_WIDTH), lambda bi, ci: (bi, ci, 0))
    tok_shape = jax.ShapeDtypeStruct((b, s, RWKV_WIDTH), F32)
    out_shape = jax.ShapeDtypeStruct((b, s, RWKV_WIDTH), BF16)
    in_specs = [col(first_col), col(first_col + 1), col(first_col + 2), col(first_col + 3)]
    args = [h3, h3, h3, h3]
    if has_vres:
        in_specs.append(tok)
        args.append(v_first)
    consts = (mu, prm, w2f, av2f, g2f, ones_blk)
    in_specs += [_const_spec(c.shape) for c in consts]
    args += list(consts)
    out = pl.pallas_call(
        functools.partial(_rwkv_kernel, has_vres=has_vres),
        grid=(b, s // t),
        in_specs=in_specs,
        out_specs=tok if has_vres else (tok, tok),
        out_shape=out_shape if has_vres else (out_shape, tok_shape),
        scratch_shapes=[
            pltpu.VMEM((SHIFT_PAD + t, 4 * BLK), F32),
            pltpu.VMEM((N_PAIRS, PAIR, PAIR), F32),
        ],
        compiler_params=_params("parallel", "arbitrary"),
        name="rwkv7",
    )(*args)
    return (out, v_first) if has_vres else out


def _merge_kernel(x_ref, ret_ref, rwkv_ref, ga_ref, gb_ref, wret_ref, wrwkv_ref, wout_ref,
                  lng_ref, lnb_ref, o_ref):
    ret_out = _mm(ret_ref[...], wret_ref[...])
    rwkv_out = _mm(rwkv_ref[...], wrwkv_ref[...])
    mixed = (_sigmoid(ga_ref[...].astype(F32)) * ret_out
             + _sigmoid(gb_ref[...].astype(F32)) * rwkv_out)
    y = DEEPNORM_ALPHA * x_ref[...] + _mm(mixed, wout_ref[...])
    o_ref[...] = _layer_norm(y, lng_ref[...], lnb_ref[...])


def _merge(x2d, ret2d, rwkv2d, h2d, w_ret, w_rwkv, w_out, ln_g, ln_b):
    m = x2d.shape[0]
    tm = min(ROW_TILE, m)
    gate_blk =(RET_COLS + RKV_COLS + LORA_COLS) // D_MODEL

    def rows(width, j=0):
        return pl.BlockSpec((tm, width), lambda i, j=j: (i, j))

    consts = (w_ret, w_rwkv, w_out, ln_g, ln_b)
    return pl.pallas_call(
        _merge_kernel,
        grid=(m // tm,),
        in_specs=[rows(D_MODEL), rows(RET_WIDTH), rows(RWKV_WIDTH), rows(D_MODEL, gate_blk),
                  rows(D_MODEL, gate_blk + 1)] + [_const_spec(c.shape) for c in consts],
        out_specs=rows(D_MODEL),
        out_shape=jax.ShapeDtypeStruct((m, D_MODEL), F32),
        compiler_params=_params("parallel"),
        name="merge_ln",
    )(x2d, ret2d, rwkv2d, h2d, h2d, *consts)


FF_CHUNK = D_FF
CONV_PAD = 8


def _ffn_kernel(x_ref, wg_ref, wv_ref, wd_ref, cw_ref, cb_ref, lng_ref, lnb_ref, o_ref,
                gate_ref):
    @pl.when(pl.program_id(1) == 0)
    def _():
        gate_ref[...] = jnp.zeros_like(gate_ref)

    t = x_ref.shape[0]
    x = x_ref[...]
    xb = x.astype(BF16)
    acc = DEEPNORM_ALPHA * x
    for c in range(D_FF // FF_CHUNK):
        cols = slice(c * FF_CHUNK, (c + 1) * FF_CHUNK)
        gate_ref[c, CONV_PAD:, :] = jnp.dot(xb, wg_ref[:, cols], preferred_element_type=F32)
        conv = (gate_ref[c, CONV_PAD - 2:CONV_PAD - 2 + t, :] * cw_ref[0:1, cols]
                + gate_ref[c, CONV_PAD - 1:CONV_PAD - 1 + t, :] * cw_ref[1:2, cols]
                + gate_ref[c, CONV_PAD:, :] * cw_ref[2:3, cols] + cb_ref[:, cols])
        gate_ref[c, CONV_PAD - 2:CONV_PAD, :] = gate_ref[c, CONV_PAD + t - 2:CONV_PAD + t, :]
        val = jnp.dot(xb, wv_ref[:, cols], preferred_element_type=F32)
        act = conv * _sigmoid(conv) * val
        acc = acc + _mm(act, wd_ref[cols, :])
    o_ref[...] = _layer_norm(acc, lng_ref[...], lnb_ref[...])


def _ffn(x3, w_gate, w_val, w_down, conv_w, conv_b, ln_g, ln_b):
    b, s, _ = x3.shape
    t = min(ROW_TILE, s)
    tok = pl.BlockSpec((None, t, D_MODEL), lambda bi, ti: (bi, ti, 0))
    consts = (w_gate, w_val, w_down, conv_w, conv_b, ln_g, ln_b)
    return pl.pallas_call(
        _ffn_kernel,
        grid=(b, s // t),
        in_specs=[tok] + [_const_spec(c.shape) for c in consts],
        out_specs=tok,
        out_shape=jax.ShapeDtypeStruct((b, s, D_MODEL), F32),
        scratch_shapes=[pltpu.VMEM((D_FF // FF_CHUNK, CONV_PAD + t, FF_CHUNK), F32)],
        compiler_params=_params("parallel", "arbitrary"),
        name="conv_mlp_ln",
    )(x3, *consts)


def _row(v):
    return v.reshape(1, -1).astype(F32)


def _pad_rows(w, start, total=LORA_COLS):
    return jnp.pad(w, ((start, total - start - w.shape[0]), (0, 0))).astype(BF16)


def kernel(x, w_in, mu_shift, w_vres_in, mu_vres, v0, v2, w0, w2, a0, a2, g2, k_k, k_a, r_k,
           lnx_g, lnx_b, w_ret_o, w_rwkv_o, w_out, ln1_g, ln1_b, w_up, conv_w, conv_b, w_down,
           ln2_g, ln2_b):
    b, s, d = x.shape
    assert d == D_MODEL and s % CHUNK == 0
    m = b * s
    cos_t, sin_t = _rope_tables(s)
    head_id = jnp.arange(RWKV_WIDTH) // RWKV_HEAD_DIM
    ones_blk = (head_id[:, None] == head_id[None, :]).astype(BF16)
    split = RET_COLS + RKV_COLS + LORA_USED
    lora_pad = LORA_COLS - LORA_USED - MV_LORA
    off_a, off_g, off_v = DECAY_LORA, DECAY_LORA + AAA_LORA, LORA_USED

    v_first = None
    for l in range(DEPTH):
        vres_w = w_vres_in[l - 1] if l > 0 else jnp.zeros((D_MODEL, MV_LORA), F32)
        vres_mu = mu_vres[l - 1] if l > 0 else jnp.zeros((MV_LORA,), F32)
        w_cat = jnp.concatenate(
            [w_in[l][:, :split], vres_w, jnp.zeros((D_MODEL, lora_pad), F32), w_in[l][:, split:]],
            axis=1).astype(BF16)
        mu_lora = jnp.concatenate(
            [mu_shift[l][RKV_COLS:], vres_mu, jnp.zeros((lora_pad,), F32)])
        mu = jnp.concatenate([mu_shift[l][:RKV_COLS].reshape(3, RWKV_WIDTH), mu_lora[None]], 0)
        v0_l = v0[l - 1] if l > 0 else jnp.zeros((RWKV_WIDTH,), F32)
        prm = jnp.stack([w0[l], a0[l], v0_l, k_k[l], k_a[l], r_k[l].reshape(-1), lnx_g[l],
                         lnx_b[l]]).astype(F32)
        w2f = _pad_rows(w2[l], 0)
        v2_l = v2[l - 1] if l > 0 else jnp.zeros((MV_LORA, RWKV_WIDTH), F32)
        av2f = jnp.concatenate([_pad_rows(a2[l], off_a), _pad_rows(v2_l, off_v)], axis=1)
        g2f = _pad_rows(g2[l], off_g)

        x2d = x.reshape(m, D_MODEL)
        h2d = _inproj(x2d, w_cat)
        h3 = h2d.reshape(b, s, N_IN)
        ret = _retention(h3, cos_t, sin_t)
        rwkv, v_first = _rwkv(h3, v_first, mu, prm, w2f, av2f, g2f, ones_blk)
        x1 = _merge(x2d, ret.reshape(m, RET_WIDTH), rwkv.reshape(m, RWKV_WIDTH), h2d,
                    w_ret_o[l].astype(BF16), w_rwkv_o[l].astype(BF16), w_out[l].astype(BF16),
                    _row(ln1_g[l]), _row(ln1_b[l]))
        x = _ffn(x1.reshape(b, s, D_MODEL), w_up[l][:, :D_FF].astype(BF16),
                 w_up[l][:, D_FF:].astype(BF16), w_down[l].astype(BF16), conv_w[l].astype(F32),
                 _row(conv_b[l]), _row(ln2_g[l]), _row(ln2_b[l]))
    return x
```

```python
import functools
import itertools
import math

import jax
import jax.numpy as jnp
from jax import lax
from jax.experimental import pallas as pl
from jax.experimental.pallas import tpu as pltpu

F32 = jnp.float32
BF16 = jnp.bfloat16

D_MODEL = 1024
DEPTH = 2
CHUNK = 64
RET_HEADS = 4
RET_HEAD_DIM = 128
RET_WIDTH = RET_HEADS * RET_HEAD_DIM
RWKV_HEADS = 8
RWKV_HEAD_DIM = 64
RWKV_WIDTH = RWKV_HEADS * RWKV_HEAD_DIM
DECAY_LORA = 64
AAA_LORA = 64
MV_LORA = 32
GATE_LORA = 160
D_FF = 2816
ROPE_BASE = 10000.0
LN_EPS = 1e-5
RET_NORM_EPS = 1e-6
LNX_EPS = 64e-5
DEEPNORM_ALPHA = (2 * DEPTH) ** 0.25

RET_COLS = 4 * RET_WIDTH
RKV_COLS = 3 * RWKV_WIDTH
LORA_COLS = 512
LORA_USED = DECAY_LORA + AAA_LORA + GATE_LORA
N_IN = RET_COLS + RKV_COLS + LORA_COLS + 2 * D_MODEL
BLK = 512

VMEM_LIMIT = 56 * 1024 * 1024
ROW_TILE = 512

NT = ((1,), (1,))
TN = ((0,), (0,))
NN = ((1,), (0,))


def _mm(a, b, dims=NN):
    return lax.dot_general(a.astype(BF16), b.astype(BF16), (dims, ((), ())),
                           preferred_element_type=F32)


def _split2(x):
    hi = x.astype(BF16)
    lo = (x - hi.astype(F32)).astype(BF16)
    return hi, lo


def _split3(x):
    hi = x.astype(BF16)
    r1 = x - hi.astype(F32)
    mid = r1.astype(BF16)
    lo = (r1 - mid.astype(F32)).astype(BF16)
    return hi, mid, lo


def _mm_exact_rhs(a_parts, b):
    acc = None
    for p in a_parts:
        t = lax.dot_general(p, b, (NN, ((), ())), preferred_element_type=F32)
        acc = t if acc is None else acc + t
    return acc


def _sigmoid(x):
    return 1.0 / (1.0 + jnp.exp(-x))


def _layer_norm(y, g, b):
    mean = jnp.mean(y, axis=-1, keepdims=True)
    yc = y - mean
    var = jnp.mean(yc * yc, axis=-1, keepdims=True)
    return yc * lax.rsqrt(var + LN_EPS) * g + b


def _params(*sem):
    return pltpu.CompilerParams(dimension_semantics=sem, vmem_limit_bytes=VMEM_LIMIT)


def _const_spec(shape):
    nd = len(shape)
    return pl.BlockSpec(shape, lambda *_: (0,) * nd, pipeline_mode=pl.Buffered(1))


def _inproj_kernel(x_ref, w_ref, o_ref):
    xb = x_ref[...].astype(BF16)
    for j in range(0, o_ref.shape[1], BLK):
        o_ref[:, j:j + BLK] = jnp.dot(xb, w_ref[:, j:j + BLK],
                                      preferred_element_type=F32).astype(o_ref.dtype)


def _inproj(x2d, w):
    m, k = x2d.shape
    n = w.shape[1]
    tm = min(ROW_TILE, m)
    return pl.pallas_call(
        _inproj_kernel,
        grid=(m // tm,),
        in_specs=[pl.BlockSpec((tm, k), lambda i: (i, 0)), _const_spec((k, n))],
        out_specs=pl.BlockSpec((tm, n), lambda i: (i, 0)),
        out_shape=jax.ShapeDtypeStruct((m, n), BF16),
        compiler_params=_params("parallel"),
        name="inproj",
    )(x2d, w)


def _rope_kernel(cos_ref, sin_ref):
    rows = cos_ref.shape[0]
    base = pl.program_id(0) * rows
    pos = (lax.broadcasted_iota(jnp.int32, (rows, RET_HEAD_DIM), 0) + base).astype(F32)
    lane = lax.broadcasted_iota(jnp.int32, (rows, RET_HEAD_DIM), 1)
    half = RET_HEAD_DIM // 2
    fidx = jnp.where(lane < half, lane, lane - half).astype(F32)
    inv_freq = jnp.exp(fidx * (-math.log(ROPE_BASE) / half))
    ang = pos * inv_freq
    cos_ref[...] = jnp.cos(ang)
    s = jnp.sin(ang)
    sin_ref[...] = jnp.where(lane < half, -s, s)


def _rope_tables(seq):
    rows = min(256, seq)
    spec = pl.BlockSpec((rows, RET_HEAD_DIM), lambda i: (i, 0))
    shape = jax.ShapeDtypeStruct((seq, RET_HEAD_DIM), F32)
    return pl.pallas_call(
        _rope_kernel, grid=(seq // rows,), in_specs=[], out_specs=(spec, spec),
        out_shape=(shape, shape), compiler_params=_params("parallel"), name="rope_tables",
    )()


def _ret_kernel(q_ref, k_ref, v_ref, g_ref, cos_ref, sin_ref, o_ref, state_ref):
    @pl.when(pl.program_id(1) == 0)
    def _():
        state_ref[...] = jnp.zeros_like(state_ref)

    ii = lax.broadcasted_iota(jnp.int32, (CHUNK, CHUNK), 0)
    jj = lax.broadcasted_iota(jnp.int32, (CHUNK, CHUNK), 1)
    dist = jnp.abs(ii - jj).astype(F32)
    tpos = lax.broadcasted_iota(jnp.int32, (CHUNK, RET_HEAD_DIM), 0).astype(F32)
    scale = RET_HEAD_DIM ** -0.5
    chunks = range(q_ref.shape[0] // CHUNK)
    heads = range(RET_HEADS)
    units = [(c, h) for c in chunks for h in heads]
    log_g = [math.log(1.0 - 2.0 ** (-5.0 - h)) for h in heads]
    intra_decay = [jnp.exp(log_g[h] * dist) for h in heads]
    q_decay = [jnp.exp(log_g[h] * (tpos + 1.0)) for h in heads]
    k_decay = [jnp.exp(log_g[h] * (CHUNK - 1.0 - tpos)) for h in heads]

    q, k, v = {}, {}, {}
    for (c, h) in units:
        rows = slice(c * CHUNK, (c + 1) * CHUNK)
        cols = slice(h * RET_HEAD_DIM, (h + 1) * RET_HEAD_DIM)
        cosv = cos_ref[rows, :]
        sinv = sin_ref[rows, :]
        qq = q_ref[rows, cols].astype(F32)
        kk = k_ref[rows, cols].astype(F32)
        q[c, h] = qq * cosv + pltpu.roll(qq, RET_HEAD_DIM // 2, 1) * sinv
        k[c, h] = (kk * cosv + pltpu.roll(kk, RET_HEAD_DIM // 2, 1) * sinv) * scale
        v[c, h] = v_ref[rows, cols]
    scores = {u: _mm(q[u], k[u], NT) * intra_decay[u[1]] for u in units}
    kv = {u: _mm(k[u] * k_decay[u[1]], v[u], TN) for u in units}
    intra = {u: _mm(scores[u], v[u]) for u in units}
    r_in = {}
    for h in heads:
        state = state_ref[h]
        for c in chunks:
            r_in[c, h] = state
            state = state * math.exp(log_g[h] * CHUNK) + kv[c, h]
        state_ref[h] = state
    cross = {u: _mm(q[u], r_in[u]) * q_decay[u[1]] for u in units}
    for (c, h) in units:
        rows = slice(c * CHUNK, (c + 1) * CHUNK)
        cols = slice(h * RET_HEAD_DIM, (h + 1) * RET_HEAD_DIM)
        o = intra[c, h] + cross[c, h]
        o = o * lax.rsqrt(jnp.mean(o * o, axis=-1, keepdims=True) + RET_NORM_EPS)
        g = g_ref[rows, cols].astype(F32)
        o_ref[rows, cols] = (o * (g * _sigmoid(g))).astype(o_ref.dtype)


def _retention(h3, cos_t, sin_t):
    b, s, _ = h3.shape
    t = min(ROW_TILE, s)

    def col(j):
        return pl.BlockSpec((None, t, BLK), lambda bi, ti, j=j: (bi, ti, j))

    tab =pl.BlockSpec((t, RET_HEAD_DIM), lambda bi, ti: (ti, 0))
    return pl.pallas_call(
        _ret_kernel,
        grid=(b, s // t),
        in_specs=[col(0), col(1), col(2), col(3), tab, tab],
        out_specs=pl.BlockSpec((None, t, RET_WIDTH), lambda bi, ti: (bi, ti, 0)),
        out_shape=jax.ShapeDtypeStruct((b, s, RET_WIDTH), BF16),
        scratch_shapes=[pltpu.VMEM((RET_HEADS, RET_HEAD_DIM, RET_HEAD_DIM), F32)],
        compiler_params=_params("parallel", "arbitrary"),
        name="retention",
    )(h3, h3, h3, h3, cos_t, sin_t)


P_W0, P_A0, P_V0, P_KK, P_KA, P_RK, P_LNG, P_LNB = range(8)
RWKV_GROUP_CHUNKS = 4
RWKV_STEP_CHUNKS = 8
RWKV_STEP = RWKV_STEP_CHUNKS * CHUNK
CARRY_ROWS = 8
PAIR = 2 * RWKV_HEAD_DIM
N_PAIRS = RWKV_HEADS // 2


INV_BLK = 16


def _tri_inverse_stages(a, units, out):
    nb = CHUNK // INV_BLK
    r16 = lax.broadcasted_iota(jnp.int32, (INV_BLK, CHUNK), 0)
    l16 = lax.broadcasted_iota(jnp.int32, (INV_BLK, CHUNK), 1)
    lane_blk = l16 // INV_BLK
    cm = [lane_blk == i for i in range(nb)]
    eye_s = jnp.where(l16 - lane_blk * INV_BLK == r16, 1.0, 0.0)
    zero16 = jnp.zeros((INV_BLK, CHUNK), F32)
    half = CHUNK // 2
    left_half = lax.broadcasted_iota(jnp.int32, (half, CHUNK), 1) < half
    zero32 = jnp.zeros((half, CHUNK), F32)

    def rows(x, i):
        return x[i * INV_BLK:(i + 1) * INV_BLK, :]

    def pick(x, i):
        return jnp.where(cm[i], x, 0.0)

    def block_diag(s):
        return jnp.concatenate([pick(s, i) for i in range(nb)], axis=0)

    p = {u: pick(rows(a[u], 0), 0) + pick(rows(a[u], 1), 1) + pick(rows(a[u], 2), 2)
            + pick(rows(a[u], 3), 3) for u in units}
    t = {u: eye_s + p[u] for u in units}
    pbd = {u: block_diag(p[u]) for u in units}
    for _ in range(3):
        p = {u: _mm(p[u], pbd[u]) for u in units}
        yield
        pbd = {u: block_diag(p[u]) for u in units}
        t = {u: t[u] + _mm(t[u], pbd[u]) for u in units}
        yield
    x = {u: _mm(pick(rows(a[u], 1), 0) + pick(rows(a[u], 3), 2), block_diag(t[u])) for u in units}
    yield
    y = {u: _mm(pick(t[u], 1) + pick(t[u], 3),
                jnp.concatenate([zero16, pick(x[u], 0), zero16, pick(x[u], 2)], axis=0)) for u in units}
    yield
    t32 = {u: jnp.concatenate([pick(t[u], 0), pick(y[u], 0) + pick(t[u], 1),
                               pick(t[u], 2), pick(y[u], 2) + pick(t[u], 3)], axis=0) for u in units}
    x2 = {u: _mm(jnp.where(left_half, a[u][half:, :], 0.0), t32[u]) for u in units}
    yield
    z = {u: _mm(t32[u][half:, :], jnp.concatenate([zero32, x2[u]], axis=0)) for u in units}
    yield
    out.update({u: jnp.concatenate([t32[u][:half, :], z[u] + t32[u][half:, :]], axis=0)
                for u in units})


def _rwkv_kernel(*refs, has_vres):
    if has_vres:
        (hr_ref, hk_ref, hv_ref, hl_ref, vfirst_ref, mu_ref, prm_ref, w2_ref, av2_ref, g2_ref,
         ones_ref, o_ref, shift_ref, state_ref) = refs
    else:
        (hr_ref, hk_ref, hv_ref, hl_ref, mu_ref, prm_ref, w2_ref, av2_ref, g2_ref,
         ones_ref, o_ref, vout_ref, shift_ref, state_ref) = refs
    rows_t = hr_ref.shape[0]
    n_chunks = rows_t // CHUNK

    @pl.when(pl.program_id(1) == 0)
    def _():
        shift_ref[...] = jnp.zeros_like(shift_ref)
        state_ref[...] = jnp.zeros_like(state_ref)

    def prm(i):
        return prm_ref[i:i + 1, :]

    def head_sum(x):
        return _mm(x, ones_ref[...])

    group_rows = min(RWKV_GROUP_CHUNKS, n_chunks) * CHUNK
    group_chunks = range(group_rows // CHUNK)
    ti = lax.broadcasted_iota(jnp.int32, (group_rows, group_rows), 0)
    si = lax.broadcasted_iota(jnp.int32, (group_rows, group_rows), 1)
    lower = jnp.where(ti >= si, jnp.where((ti // CHUNK) == (si // CHUNK), 1.0, 0.0), 0.0).astype(BF16)
    pairs = range(N_PAIRS)

    shift_mat = jnp.where(ti == si + 1, 1.0, 0.0).astype(BF16)
    first8 = lax.broadcasted_iota(jnp.int32, (8, BLK), 0) == 0

    def prep(g, carry):
        rows = slice(g * group_rows, (g + 1) * group_rows)
        z = []
        for i, ref in enumerate((hr_ref, hk_ref, hv_ref, hl_ref)):
            cur_b = ref[rows, :]
            cur = cur_b.astype(F32)
            prev = lax.dot_general(shift_mat, cur_b, (NN, ((), ())), preferred_element_type=F32)
            prev = jnp.concatenate([jnp.where(first8, carry[i], prev[:8, :]), prev[8:, :]], axis=0)
            z.append(cur + mu_ref[i:i + 1, :] * (prev - cur))
            carry[i] = cur[group_rows - 1:group_rows, :]
        r, k, v, lora = z
        dw = prm(P_W0) + _mm(jnp.tanh(lora), w2_ref[...])
        log_decay = -math.exp(-0.5) * _sigmoid(dw)
        av = _mm(lora, av2_ref[...])
        a = _sigmoid(prm(P_A0) + av[:, :RWKV_WIDTH])
        if has_vres:
            v = v + (vfirst_ref[rows, :] - v) * _sigmoid(prm(P_V0) + av[:, RWKV_WIDTH:])
        else:
            vout_ref[rows, :] = v
        gate = _mm(_sigmoid(lora), g2_ref[...])
        kk = k * prm(P_KK)
        kk = kk * lax.rsqrt(jnp.maximum(head_sum(kk * kk), 1e-24))
        k = k * (1.0 + (a - 1.0) * prm(P_KA))
        cum = None
        for part in _split3(log_decay):
            t = lax.dot_general(lower, part, (NN, ((), ())), preferred_element_type=F32)
            cum = t if cum is None else cum + t
        e_in = jnp.exp(cum)
        e_neg = jnp.exp(-cum)
        b_t = kk * a * e_neg
        k_t = k * e_neg
        bkt, gcol = [], []
        for c in group_chunks:
            crows = slice(c * CHUNK, (c + 1) * CHUNK)
            bk_t = jnp.concatenate([b_t[crows, :], k_t[crows, :]], axis=0).T
            g_t = jnp.broadcast_to(e_in[(c + 1) * CHUNK - 1:(c + 1) * CHUNK, :], (PAIR, RWKV_WIDTH)).T
            bkt.append([bk_t[p * PAIR:(p + 1) * PAIR, :] for p in pairs])
            gcol.append([g_t[p * PAIR:(p + 1) * PAIR, :] for p in pairs])
        return dict(r=r, k=k, v=v, gate=gate, a_t=-kk * jnp.exp(cum - log_decay), r_t=r * e_in,
                    bkt=bkt, gcol=gcol)

    row = lax.broadcasted_iota(jnp.int32, (PAIR, PAIR), 0)
    lane = lax.broadcasted_iota(jnp.int32, (PAIR, PAIR), 1)
    hd = RWKV_HEAD_DIM
    row_hi = jnp.where(row < hd, 0, 1)
    lane_hi = jnp.where(lane < hd, 0, 1)
    tri_mask = (row - hd * row_hi + row_hi) > (lane - hd * lane_hi)
    bd_mask = row_hi == lane_hi
    lane_lo = lax.broadcasted_iota(jnp.int32, (CHUNK, PAIR), 1) < hd
    zeros = jnp.zeros((CHUNK, PAIR), F32)
    units = [(c, p, e) for c in group_chunks for p in pairs for e in range(2)]
    cps = [(c, p) for c in group_chunks for p in pairs]

    def blk(x, c, p):
        return x[c * CHUNK:(c + 1) * CHUNK, p * PAIR:(p + 1) * PAIR]

    def state_free_part(d, out):
        aa = {}
        for (c, p) in cps:
            bkt = d["bkt"][c][p]
            rhs = jnp.concatenate([jnp.concatenate([bkt[:hd, :], zeros], axis=0),
                                   jnp.concatenate([zeros, bkt[hd:, :]], axis=0)], axis=1)
            res = _mm(jnp.concatenate([blk(d["a_t"], c, p), blk(d["r_t"], c, p)], axis=0), rhs)
            for e in range(2):
                aa[c, p, e] = jnp.where(tri_mask, res[:, e * PAIR:(e + 1) * PAIR], 0.0)
        yield
        tinv = {}
        yield from _tri_inverse_stages({u: aa[u][:CHUNK, :CHUNK] for u in units}, units, tinv)
        zv = {cp: jnp.concatenate([zeros, blk(d["v"], *cp)], axis=0) for cp in cps}
        akv = {(c, p, e): _mm(aa[c, p, e][:CHUNK, :], zv[c, p]) for (c, p, e) in units}
        yield
        tw = {(c, p, e): _mm(tinv[c, p, e], jnp.concatenate([blk(d["a_t"], c, p), akv[c, p, e]], axis=1))
              for (c, p, e) in units}
        yield
        out["aa"] = aa
        out["ut"] = {(c, p): jnp.where(lane_lo, tw[c, p, 0][:, PAIR:], tw[c, p, 1][:, PAIR:])
                     for (c, p) in cps}
        out["wr"] = {(c, p): jnp.concatenate(
            [jnp.where(lane_lo, tw[c, p, 0][:, :PAIR], tw[c, p, 1][:, :PAIR]), blk(d["r_t"], c, p)],
            axis=0) for (c, p) in cps}
        out["kv"] = {(c, p): jnp.where(bd_mask, _mm(d["bkt"][c][p], zv[c, p]), 0.0) for (c, p) in cps}
        yield

    def state_part(g, d, q, hs):
        inv_n = 1.0 / RWKV_HEAD_DIM
        bonus = head_sum(d["r"] * d["k"] * prm(P_RK)) * d["v"]

        def finish(c, y):
            crows = slice(c * CHUNK, (c + 1) * CHUNK)
            orows = slice(g * group_rows + c * CHUNK, g * group_rows + (c + 1) * CHUNK)
            yc = y - head_sum(y) * inv_n
            yield
            yn = yc * lax.rsqrt(head_sum(yc * yc) * inv_n + LNX_EPS)
            yn = yn * prm(P_LNG) + prm(P_LNB)
            o_ref[orows, :] = ((yn + bonus[crows, :]) * d["gate"][crows, :]).astype(o_ref.dtype)
            yield

        tail = iter(())
        for c in group_chunks:
            x = [_mm(q["wr"][c, p], hs[p]) for p in pairs]
            next(tail, None)
            yield
            u = [x[p][:CHUNK, :] + q["ut"][c, p] for p in pairs]
            hs[:] = [d["gcol"][c][p] * (hs[p] + q["kv"][c, p]
                                        + jnp.where(bd_mask, _mm(d["bkt"][c][p][:, :CHUNK], u[p]), 0.0))
                     for p in pairs]
            y_p = []
            for p in pairs:
                uv = jnp.concatenate([u[p], blk(d["v"], c, p)], axis=0)
                y0 = _mm(q["aa"][c, p, 0][CHUNK:, :], uv)
                y1 = _mm(q["aa"][c, p, 1][CHUNK:, :], uv)
                y_p.append(x[p][CHUNK:, :] + jnp.where(lane_lo, y0, y1))
            y = jnp.concatenate(y_p, axis=1)
            next(tail, None)
            yield
            tail = finish(c, y)
        for _ in tail:
            yield

    hs = [state_ref[p] for p in pairs]
    n_groups = rows_t // group_rows
    pending = ()
    carry = [shift_ref[0:1, i * BLK:(i + 1) * BLK] for i in range(4)]
    d = prep(0, carry)
    for g in range(n_groups):
        q = {}
        stages = state_free_part(d, q)
        next(stages)
        d_next = prep(g + 1, carry) if g + 1 < n_groups else None
        for _ in itertools.zip_longest(stages, pending):
            pass
        pending = state_part(g, d, q, hs)
        d = d_next
    for _ in pending:
        pass
    for p in pairs:
        state_ref[p] = hs[p]
    for i in range(4):
        shift_ref[0:1, i * BLK:(i + 1) * BLK] = carry[i]


def _rwkv(h3, v_first, mu, prm, w2f, av2f, g2f, ones_blk):
    b, s, _ = h3.shape
    has_vres = v_first is not None
    first_col = RET_COLS // BLK
    t = min(RWKV_STEP, s)

    def col(j):
        return pl.BlockSpec((None, t, BLK), lambda bi, ci, j=j: (bi, ci, j))

    tok = pl.BlockSpec((None, t, RWKV_WIDTH), lambda bi, ci: (bi, ci, 0))
    tok_shape = jax.ShapeDtypeStruct((b, s, RWKV_WIDTH), F32)
    out_shape = jax.ShapeDtypeStruct((b, s, RWKV_WIDTH), BF16)
    in_specs = [col(first_col), col(first_col + 1), col(first_col + 2), col(first_col + 3)]
    args = [h3, h3, h3, h3]
    if has_vres:
        in_specs.append(tok)
        args.append(v_first)
    consts = (mu, prm, w2f, av2f, g2f, ones_blk)
    in_specs += [_const_spec(c.shape) for c in consts]
    args += list(consts)
    out = pl.pallas_call(
        functools.partial(_rwkv_kernel, has_vres=has_vres),
        grid=(b, s // t),
        in_specs=in_specs,
        out_specs=tok if has_vres else (tok, tok),
        out_shape=out_shape if has_vres else (out_shape, tok_shape),
        scratch_shapes=[
            pltpu.VMEM((CARRY_ROWS, 4 * BLK), F32),
            pltpu.VMEM((N_PAIRS, PAIR, PAIR), F32),
        ],
        compiler_params=_params("parallel", "arbitrary"),
        name="rwkv7",
    )(*args)
    return (out, v_first) if has_vres else out


def _merge_kernel(x_ref, ret_ref, rwkv_ref, ga_ref, gb_ref, wret_ref, wrwkv_ref, wout_ref,
                  lng_ref, lnb_ref, o_ref):
    ret_out = _mm(ret_ref[...], wret_ref[...])
    rwkv_out = _mm(rwkv_ref[...], wrwkv_ref[...])
    mixed = (_sigmoid(ga_ref[...].astype(F32)) * ret_out
             + _sigmoid(gb_ref[...].astype(F32)) * rwkv_out)
    y = DEEPNORM_ALPHA * x_ref[...] + _mm(mixed, wout_ref[...])
    o_ref[...] = _layer_norm(y, lng_ref[...], lnb_ref[...])


def _merge(x2d, ret2d, rwkv2d, h2d, w_ret, w_rwkv, w_out, ln_g, ln_b):
    m = x2d.shape[0]
    tm = min(ROW_TILE, m)
    gate_blk =(RET_COLS + RKV_COLS + LORA_COLS) // D_MODEL

    def rows(width, j=0):
        return pl.BlockSpec((tm, width), lambda i, j=j: (i, j))

    consts = (w_ret, w_rwkv, w_out, ln_g, ln_b)
    return pl.pallas_call(
        _merge_kernel,
        grid=(m // tm,),
        in_specs=[rows(D_MODEL), rows(RET_WIDTH), rows(RWKV_WIDTH), rows(D_MODEL, gate_blk),
                  rows(D_MODEL, gate_blk + 1)] + [_const_spec(c.shape) for c in consts],
        out_specs=rows(D_MODEL),
        out_shape=jax.ShapeDtypeStruct((m, D_MODEL), F32),
        compiler_params=_params("parallel"),
        name="merge_ln",
    )(x2d, ret2d, rwkv2d, h2d, h2d, *consts)


FF_CHUNK = D_FF
CONV_PAD = 8


def _ffn_kernel(x_ref, wg_ref, wv_ref, wd_ref, cw_ref, cb_ref, lng_ref, lnb_ref, o_ref,
                gate_ref):
    @pl.when(pl.program_id(1) == 0)
    def _():
        gate_ref[...] = jnp.zeros_like(gate_ref)

    t = x_ref.shape[0]
    x = x_ref[...]
    xb = x.astype(BF16)
    acc = DEEPNORM_ALPHA * x
    for c in range(D_FF // FF_CHUNK):
        cols = slice(c * FF_CHUNK, (c + 1) * FF_CHUNK)
        gate_ref[c, CONV_PAD:, :] = jnp.dot(xb, wg_ref[:, cols], preferred_element_type=F32)
        conv = (gate_ref[c, CONV_PAD - 2:CONV_PAD - 2 + t, :] * cw_ref[0:1, cols]
                + gate_ref[c, CONV_PAD - 1:CONV_PAD - 1 + t, :] * cw_ref[1:2, cols]
                + gate_ref[c, CONV_PAD:, :] * cw_ref[2:3, cols] + cb_ref[:, cols])
        gate_ref[c, CONV_PAD - 2:CONV_PAD, :] = gate_ref[c, CONV_PAD + t - 2:CONV_PAD + t, :]
        val = jnp.dot(xb, wv_ref[:, cols], preferred_element_type=F32)
        act = conv * _sigmoid(conv) * val
        acc = acc + _mm(act, wd_ref[cols, :])
    o_ref[...] = _layer_norm(acc, lng_ref[...], lnb_ref[...])


def _ffn(x3, w_gate, w_val, w_down, conv_w, conv_b, ln_g, ln_b):
    b, s, _ = x3.shape
    t = min(ROW_TILE, s)
    tok = pl.BlockSpec((None, t, D_MODEL), lambda bi, ti: (bi, ti, 0))
    consts = (w_gate, w_val, w_down, conv_w, conv_b, ln_g, ln_b)
    return pl.pallas_call(
        _ffn_kernel,
        grid=(b, s // t),
        in_specs=[tok] + [_const_spec(c.shape) for c in consts],
        out_specs=tok,
        out_shape=jax.ShapeDtypeStruct((b, s, D_MODEL), F32),
        scratch_shapes=[pltpu.VMEM((D_FF // FF_CHUNK, CONV_PAD + t, FF_CHUNK), F32)],
        compiler_params=_params("parallel", "arbitrary"),
        name="conv_mlp_ln",
    )(x3, *consts)


def _row(v):
    return v.reshape(1, -1).astype(F32)


def _pad_rows(w, start, total=LORA_COLS):
    return jnp.pad(w, ((start, total - start - w.shape[0]), (0, 0))).astype(BF16)


def kernel(x, w_in, mu_shift, w_vres_in, mu_vres, v0, v2, w0, w2, a0, a2, g2, k_k, k_a, r_k,
           lnx_g, lnx_b, w_ret_o, w_rwkv_o, w_out, ln1_g, ln1_b, w_up, conv_w, conv_b, w_down,
           ln2_g, ln2_b):
    b, s, d = x.shape
    assert d == D_MODEL and s % CHUNK == 0
    m = b * s
    cos_t, sin_t = _rope_tables(s)
    head_id = jnp.arange(RWKV_WIDTH) // RWKV_HEAD_DIM
    ones_blk = (head_id[:, None] == head_id[None, :]).astype(BF16)
    split = RET_COLS + RKV_COLS + LORA_USED
    lora_pad = LORA_COLS - LORA_USED - MV_LORA
    off_a, off_g, off_v = DECAY_LORA, DECAY_LORA + AAA_LORA, LORA_USED

    v_first = None
    for l in range(DEPTH):
        vres_w = w_vres_in[l - 1] if l > 0 else jnp.zeros((D_MODEL, MV_LORA), F32)
        vres_mu = mu_vres[l - 1] if l > 0 else jnp.zeros((MV_LORA,), F32)
        w_cat = jnp.concatenate(
            [w_in[l][:, :split], vres_w, jnp.zeros((D_MODEL, lora_pad), F32), w_in[l][:, split:]],
            axis=1).astype(BF16)
        mu_lora = jnp.concatenate(
            [mu_shift[l][RKV_COLS:], vres_mu, jnp.zeros((lora_pad,), F32)])
        mu = jnp.concatenate([mu_shift[l][:RKV_COLS].reshape(3, RWKV_WIDTH), mu_lora[None]], 0)
        v0_l = v0[l - 1] if l > 0 else jnp.zeros((RWKV_WIDTH,), F32)
        prm = jnp.stack([w0[l], a0[l], v0_l, k_k[l], k_a[l], r_k[l].reshape(-1), lnx_g[l],
                         lnx_b[l]]).astype(F32)
        w2f = _pad_rows(w2[l], 0)
        v2_l = v2[l - 1] if l > 0 else jnp.zeros((MV_LORA, RWKV_WIDTH), F32)
        av2f = jnp.concatenate([_pad_rows(a2[l], off_a), _pad_rows(v2_l, off_v)], axis=1)
        g2f = _pad_rows(g2[l], off_g)

        x2d = x.reshape(m, D_MODEL)
        h2d = _inproj(x2d, w_cat)
        h3 = h2d.reshape(b, s, N_IN)
        ret = _retention(h3, cos_t, sin_t)
        rwkv, v_first = _rwkv(h3, v_first, mu, prm, w2f, av2f, g2f, ones_blk)
        x1 = _merge(x2d, ret.reshape(m, RET_WIDTH), rwkv.reshape(m, RWKV_WIDTH), h2d,
                    w_ret_o[l].astype(BF16), w_rwkv_o[l].astype(BF16), w_out[l].astype(BF16),
                    _row(ln1_g[l]), _row(ln1_b[l]))
        x = _ffn(x1.reshape(b, s, D_MODEL), w_up[l][:, :D_FF].astype(BF16),
                 w_up[l][:, D_FF:].astype(BF16), w_down[l].astype(BF16), conv_w[l].astype(F32),
                 _row(conv_b[l]), _row(ln2_g[l]), _row(ln2_b[l]))
    return x
```

```python
import functools
import itertools
import math

import jax
import jax.numpy as jnp
from jax import lax
from jax.experimental import pallas as pl
from jax.experimental.pallas import tpu as pltpu

F32 = jnp.float32
BF16 = jnp.bfloat16

D_MODEL = 1024
DEPTH = 2
CHUNK = 64
RET_HEADS = 4
RET_HEAD_DIM = 128
RET_WIDTH = RET_HEADS * RET_HEAD_DIM
RWKV_HEADS = 8
RWKV_HEAD_DIM = 64
RWKV_WIDTH = RWKV_HEADS * RWKV_HEAD_DIM
DECAY_LORA = 64
AAA_LORA = 64
MV_LORA = 32
GATE_LORA = 160
D_FF = 2816
ROPE_BASE = 10000.0
LN_EPS = 1e-5
RET_NORM_EPS = 1e-6
LNX_EPS = 64e-5
DEEPNORM_ALPHA = (2 * DEPTH) ** 0.25

RET_COLS = 4 * RET_WIDTH
RKV_COLS = 3 * RWKV_WIDTH
LORA_COLS = 512
LORA_USED = DECAY_LORA + AAA_LORA + GATE_LORA
LORA_WIN_A = 128
LORA_WIN_G = (128, 384)
LORA_WIN_V = (256, 384)
N_IN = RET_COLS + RKV_COLS + LORA_COLS + 2 * D_MODEL
BLK = 512

VMEM_LIMIT = 56 * 1024 * 1024
ROW_TILE = 512

NT = ((1,), (1,))
TN = ((0,), (0,))
NN = ((1,), (0,))


def _mm(a, b, dims=NN):
    return lax.dot_general(a.astype(BF16), b.astype(BF16), (dims, ((), ())),
                           preferred_element_type=F32)


def _split2(x):
    hi = x.astype(BF16)
    lo = (x - hi.astype(F32)).astype(BF16)
    return hi, lo


def _sigmoid(x):
    return 1.0 / (1.0 + jnp.exp(-x))


def _layer_norm(y, g, b):
    mean = jnp.mean(y, axis=-1, keepdims=True)
    yc = y - mean
    var = jnp.mean(yc * yc, axis=-1, keepdims=True)
    return yc * lax.rsqrt(var + LN_EPS) * g + b


def _params(*sem):
    return pltpu.CompilerParams(dimension_semantics=sem, vmem_limit_bytes=VMEM_LIMIT)


def _const_spec(shape):
    nd = len(shape)
    return pl.BlockSpec(shape, lambda *_: (0,) * nd, pipeline_mode=pl.Buffered(1))


def _inproj_kernel(x_ref, *refs):
    w_refs, o_ref = refs[:-1], refs[-1]
    xb = x_ref[...].astype(BF16)
    base = 0
    for w_ref in w_refs:
        for j in range(0, w_ref.shape[1], BLK):
            o_ref[:, base + j:base + j + BLK] = jnp.dot(
                xb, w_ref[:, j:j + BLK], preferred_element_type=F32).astype(o_ref.dtype)
        base += w_ref.shape[1]


def _inproj(x2d, weights):
    m, k = x2d.shape
    n = sum(w.shape[1] for w in weights)
    tm = min(ROW_TILE, m)
    return pl.pallas_call(
        _inproj_kernel,
        grid=(m // tm,),
        in_specs=[pl.BlockSpec((tm, k), lambda i: (i, 0))] + [_const_spec(w.shape) for w in weights],
        out_specs=pl.BlockSpec((tm, n), lambda i: (i, 0)),
        out_shape=jax.ShapeDtypeStruct((m, n), BF16),
        compiler_params=_params("parallel"),
        name="inproj",
    )(x2d, *weights)


def _rope_kernel(cos_ref, sin_ref):
    rows = cos_ref.shape[0]
    base = pl.program_id(0) * rows
    pos = (lax.broadcasted_iota(jnp.int32, (rows, RET_HEAD_DIM), 0) + base).astype(F32)
    lane = lax.broadcasted_iota(jnp.int32, (rows, RET_HEAD_DIM), 1)
    half = RET_HEAD_DIM // 2
    fidx = jnp.where(lane < half, lane, lane - half).astype(F32)
    inv_freq = jnp.exp(fidx * (-math.log(ROPE_BASE) / half))
    ang = pos * inv_freq
    cos_ref[...] = jnp.cos(ang)
    s = jnp.sin(ang)
    sin_ref[...] = jnp.where(lane < half, -s, s)


def _rope_tables(seq):
    rows = min(256, seq)
    spec = pl.BlockSpec((rows, RET_HEAD_DIM), lambda i: (i, 0))
    shape = jax.ShapeDtypeStruct((seq, RET_HEAD_DIM), F32)
    return pl.pallas_call(
        _rope_kernel, grid=(seq // rows,), in_specs=[], out_specs=(spec, spec),
        out_shape=(shape, shape), compiler_params=_params("parallel"), name="rope_tables",
    )()


def _ret_kernel(q_ref, k_ref, v_ref, g_ref, cos_ref, sin_ref, o_ref, state_ref):
    @pl.when(pl.program_id(1) == 0)
    def _():
        state_ref[...] = jnp.zeros_like(state_ref)

    ii = lax.broadcasted_iota(jnp.int32, (CHUNK, CHUNK), 0)
    jj = lax.broadcasted_iota(jnp.int32, (CHUNK, CHUNK), 1)
    dist = jnp.abs(ii - jj).astype(F32)
    tpos = lax.broadcasted_iota(jnp.int32, (CHUNK, RET_HEAD_DIM), 0).astype(F32)
    scale = RET_HEAD_DIM ** -0.5
    chunks = range(q_ref.shape[0] // CHUNK)
    heads = range(RET_HEADS)
    units = [(c, h) for c in chunks for h in heads]
    log_g = [math.log(1.0 - 2.0 ** (-5.0 - h)) for h in heads]
    intra_decay = [jnp.exp(log_g[h] * dist) for h in heads]
    q_decay = [jnp.exp(log_g[h] * (tpos + 1.0)) for h in heads]
    k_decay = [jnp.exp(log_g[h] * (CHUNK - 1.0 - tpos)) for h in heads]

    q, k, v = {}, {}, {}
    for (c, h) in units:
        rows = slice(c * CHUNK, (c + 1) * CHUNK)
        cols = slice(h * RET_HEAD_DIM, (h + 1) * RET_HEAD_DIM)
        cosv = cos_ref[rows, :]
        sinv = sin_ref[rows, :]
        qq = q_ref[rows, cols].astype(F32)
        kk = k_ref[rows, cols].astype(F32)
        q[c, h] = qq * cosv + pltpu.roll(qq, RET_HEAD_DIM // 2, 1) * sinv
        k[c, h] = (kk * cosv + pltpu.roll(kk, RET_HEAD_DIM // 2, 1) * sinv) * scale
        v[c, h] = v_ref[rows, cols]
    scores = {u: _mm(q[u], k[u], NT) * intra_decay[u[1]] for u in units}
    kv = {u: _mm(k[u] * k_decay[u[1]], v[u], TN) for u in units}
    intra = {u: _mm(scores[u], v[u]) for u in units}
    r_in = {}
    for h in heads:
        state = state_ref[h]
        for c in chunks:
            r_in[c, h] = state
            state = state * math.exp(log_g[h] * CHUNK) + kv[c, h]
        state_ref[h] = state
    cross = {u: _mm(q[u], r_in[u]) * q_decay[u[1]] for u in units}
    for (c, h) in units:
        rows = slice(c * CHUNK, (c + 1) * CHUNK)
        cols = slice(h * RET_HEAD_DIM, (h + 1) * RET_HEAD_DIM)
        o = intra[c, h] + cross[c, h]
        o = o * lax.rsqrt(jnp.mean(o * o, axis=-1, keepdims=True) + RET_NORM_EPS)
        g = g_ref[rows, cols].astype(F32)
        o_ref[rows, cols] = (o * (g * _sigmoid(g))).astype(o_ref.dtype)


def _retention(h3, cos_t, sin_t):
    b, s, _ = h3.shape
    t = min(ROW_TILE, s)

    def col(j):
        return pl.BlockSpec((None, t, BLK), lambda bi, ti, j=j: (bi, ti, j))

    tab =pl.BlockSpec((t, RET_HEAD_DIM), lambda bi, ti: (ti, 0))
    return pl.pallas_call(
        _ret_kernel,
        grid=(b, s // t),
        in_specs=[col(0), col(1), col(2), col(3), tab, tab],
        out_specs=pl.BlockSpec((None, t, RET_WIDTH), lambda bi, ti: (bi, ti, 0)),
        out_shape=jax.ShapeDtypeStruct((b, s, RET_WIDTH), BF16),
        scratch_shapes=[pltpu.VMEM((RET_HEADS, RET_HEAD_DIM, RET_HEAD_DIM), F32)],
        compiler_params=_params("parallel", "arbitrary"),
        name="retention",
    )(h3, h3, h3, h3, cos_t, sin_t)


P_W0, P_A0, P_V0, P_KK, P_KA, P_RK, P_LNG, P_LNB = range(8)
RWKV_GROUP_CHUNKS = 4
RWKV_STEP_CHUNKS = 8
RWKV_STEP = RWKV_STEP_CHUNKS * CHUNK
CARRY_ROWS = 8
PAIR = 2 * RWKV_HEAD_DIM
N_PAIRS = RWKV_HEADS // 2


INV_BLK = 16


def _tri_inverse_stages(a, units, out):
    nb = CHUNK // INV_BLK
    r16 = lax.broadcasted_iota(jnp.int32, (INV_BLK, CHUNK), 0)
    l16 = lax.broadcasted_iota(jnp.int32, (INV_BLK, CHUNK), 1)
    lane_blk = l16 // INV_BLK
    cm = [lane_blk == i for i in range(nb)]
    eye_s = jnp.where(l16 - lane_blk * INV_BLK == r16, 1.0, 0.0)
    zero16 = jnp.zeros((INV_BLK, CHUNK), F32)
    half = CHUNK // 2
    left_half = lax.broadcasted_iota(jnp.int32, (half, CHUNK), 1) < half
    zero32 = jnp.zeros((half, CHUNK), F32)

    def rows(x, i):
        return x[i * INV_BLK:(i + 1) * INV_BLK, :]

    def pick(x, i):
        return jnp.where(cm[i], x, 0.0)

    def block_diag(s):
        return jnp.concatenate([pick(s, i) for i in range(nb)], axis=0)

    p = {u: pick(rows(a[u], 0), 0) + pick(rows(a[u], 1), 1) + pick(rows(a[u], 2), 2)
            + pick(rows(a[u], 3), 3) for u in units}
    t = {u: eye_s + p[u] for u in units}
    pbd = {u: block_diag(p[u]) for u in units}
    for _ in range(3):
        p = {u: _mm(p[u], pbd[u]) for u in units}
        yield
        pbd = {u: block_diag(p[u]) for u in units}
        t = {u: t[u] + _mm(t[u], pbd[u]) for u in units}
        yield
    x = {u: _mm(pick(rows(a[u], 1), 0) + pick(rows(a[u], 3), 2), block_diag(t[u])) for u in units}
    yield
    y = {u: _mm(pick(t[u], 1) + pick(t[u], 3),
                jnp.concatenate([zero16, pick(x[u], 0), zero16, pick(x[u], 2)], axis=0)) for u in units}
    yield
    t32 = {u: jnp.concatenate([pick(t[u], 0), pick(y[u], 0) + pick(t[u], 1),
                               pick(t[u], 2), pick(y[u], 2) + pick(t[u], 3)], axis=0) for u in units}
    x2 = {u: _mm(jnp.where(left_half, a[u][half:, :], 0.0), t32[u]) for u in units}
    yield
    z = {u: _mm(t32[u][half:, :], jnp.concatenate([zero32, x2[u]], axis=0)) for u in units}
    yield
    out.update({u: jnp.concatenate([t32[u][:half, :], z[u] + t32[u][half:, :]], axis=0)
                for u in units})


def _rwkv_kernel(*refs, has_vres):
    if has_vres:
        (hr_ref, hk_ref, hv_ref, hl_ref, vfirst_ref, mu_ref, prm_ref, w2_ref, a2_ref, v2_ref, g2_ref,
         ones_ref, o_ref, shift_ref, state_ref) = refs
    else:
        (hr_ref, hk_ref, hv_ref, hl_ref, mu_ref, prm_ref, w2_ref, a2_ref, v2_ref, g2_ref,
         ones_ref, o_ref, vout_ref, shift_ref, state_ref) = refs
    rows_t = hr_ref.shape[0]
    n_chunks = rows_t // CHUNK

    @pl.when(pl.program_id(1) == 0)
    def _():
        shift_ref[...] = jnp.zeros_like(shift_ref)
        state_ref[...] = jnp.zeros_like(state_ref)

    def prm(i):
        return prm_ref[i:i + 1, :]

    def head_sum(x):
        return _mm(x, ones_ref[...])

    group_rows = min(RWKV_GROUP_CHUNKS, n_chunks) * CHUNK
    group_chunks = range(group_rows // CHUNK)
    ti = lax.broadcasted_iota(jnp.int32, (group_rows, group_rows), 0)
    si = lax.broadcasted_iota(jnp.int32, (group_rows, group_rows), 1)
    lower = jnp.where(ti >= si, jnp.where((ti // CHUNK) == (si // CHUNK), 1.0, 0.0), 0.0).astype(BF16)
    pairs = range(N_PAIRS)

    shift_mat = jnp.where(ti == si + 1, 1.0, 0.0).astype(BF16)
    first8 = lax.broadcasted_iota(jnp.int32, (8, BLK), 0) == 0

    def prep(g, carry):
        rows = slice(g * group_rows, (g + 1) * group_rows)
        z = []
        for i, ref in enumerate((hr_ref, hk_ref, hv_ref, hl_ref)):
            cur_b = ref[rows, :]
            cur = cur_b.astype(F32)
            prev = lax.dot_general(shift_mat, cur_b, (NN, ((), ())), preferred_element_type=F32)
            prev = jnp.concatenate([jnp.where(first8, carry[i], prev[:8, :]), prev[8:, :]], axis=0)
            z.append(cur + mu_ref[i:i + 1, :] * (prev - cur))
            carry[i] = cur[group_rows - 1:group_rows, :]
        r, k, v, lora = z
        lora_wa = lora[:, :LORA_WIN_A]
        dw = prm(P_W0) + _mm(jnp.tanh(lora_wa), w2_ref[...])
        log_decay = -math.exp(-0.5) * _sigmoid(dw)
        a = _sigmoid(prm(P_A0) + _mm(lora_wa, a2_ref[...]))
        if has_vres:
            v_mix = _mm(lora[:, LORA_WIN_V[0]:LORA_WIN_V[1]], v2_ref[...])
            v = v + (vfirst_ref[rows, :] - v) * _sigmoid(prm(P_V0) + v_mix)
        else:
            vout_ref[rows, :] = v
        gate = _mm(_sigmoid(lora[:, LORA_WIN_G[0]:LORA_WIN_G[1]]), g2_ref[...])
        kk = k * prm(P_KK)
        kk = kk * lax.rsqrt(jnp.maximum(head_sum(kk * kk), 1e-24))
        k = k * (1.0 + (a - 1.0) * prm(P_KA))
        cum = None
        for part in _split2(log_decay):
            t = lax.dot_general(lower, part, (NN, ((), ())), preferred_element_type=F32)
            cum = t if cum is None else cum + t
        e_in = jnp.exp(cum)
        e_neg = jnp.exp(-cum)
        b_t = kk * a * e_neg
        k_t = k * e_neg
        bkt, gcol = [], []
        for c in group_chunks:
            crows = slice(c * CHUNK, (c + 1) * CHUNK)
            bk_t = jnp.concatenate([b_t[crows, :], k_t[crows, :]], axis=0).T
            g_t = jnp.broadcast_to(e_in[(c + 1) * CHUNK - 1:(c + 1) * CHUNK, :], (PAIR, RWKV_WIDTH)).T
            bkt.append([bk_t[p * PAIR:(p + 1) * PAIR, :] for p in pairs])
            gcol.append([g_t[p * PAIR:(p + 1) * PAIR, :] for p in pairs])
        return dict(r=r, k=k, v=v, gate=gate, a_t=-kk * jnp.exp(cum - log_decay), r_t=r * e_in,
                    bkt=bkt, gcol=gcol)

    row = lax.broadcasted_iota(jnp.int32, (PAIR, PAIR), 0)
    lane = lax.broadcasted_iota(jnp.int32, (PAIR, PAIR), 1)
    hd = RWKV_HEAD_DIM
    row_hi = jnp.where(row < hd, 0, 1)
    lane_hi = jnp.where(lane < hd, 0, 1)
    tri_mask = (row - hd * row_hi + row_hi) > (lane - hd * lane_hi)
    bd_mask = row_hi == lane_hi
    lane_lo = lax.broadcasted_iota(jnp.int32, (CHUNK, PAIR), 1) < hd
    zeros = jnp.zeros((CHUNK, PAIR), F32)
    units = [(c, p, e) for c in group_chunks for p in pairs for e in range(2)]
    cps = [(c, p) for c in group_chunks for p in pairs]

    def blk(x, c, p):
        return x[c * CHUNK:(c + 1) * CHUNK, p * PAIR:(p + 1) * PAIR]

    def state_free_part(d, out):
        aa = {}
        for (c, p) in cps:
            bkt = d["bkt"][c][p]
            rhs = jnp.concatenate([jnp.concatenate([bkt[:hd, :], zeros], axis=0),
                                   jnp.concatenate([zeros, bkt[hd:, :]], axis=0)], axis=1)
            res = _mm(jnp.concatenate([blk(d["a_t"], c, p), blk(d["r_t"], c, p)], axis=0), rhs)
            for e in range(2):
                aa[c, p, e] = jnp.where(tri_mask, res[:, e * PAIR:(e + 1) * PAIR], 0.0)
        yield
        tinv = {}
        yield from _tri_inverse_stages({u: aa[u][:CHUNK, :CHUNK] for u in units}, units, tinv)
        zv = {cp: jnp.concatenate([zeros, blk(d["v"], *cp)], axis=0) for cp in cps}
        akv = {(c, p, e): _mm(aa[c, p, e][:CHUNK, :], zv[c, p]) for (c, p, e) in units}
        yield
        tw = {(c, p, e): _mm(tinv[c, p, e], jnp.concatenate([blk(d["a_t"], c, p), akv[c, p, e]], axis=1))
              for (c, p, e) in units}
        yield
        out["aa"] = aa
        out["ut"] = {(c, p): jnp.where(lane_lo, tw[c, p, 0][:, PAIR:], tw[c, p, 1][:, PAIR:])
                     for (c, p) in cps}
        out["wr"] = {(c, p): jnp.concatenate(
            [jnp.where(lane_lo, tw[c, p, 0][:, :PAIR], tw[c, p, 1][:, :PAIR]), blk(d["r_t"], c, p)],
            axis=0) for (c, p) in cps}
        out["kv"] = {(c, p): jnp.where(bd_mask, _mm(d["bkt"][c][p], zv[c, p]), 0.0) for (c, p) in cps}
        yield

    def state_part(g, d, q, hs):
        inv_n = 1.0 / RWKV_HEAD_DIM
        bonus = head_sum(d["r"] * d["k"] * prm(P_RK)) * d["v"]

        def finish(c, y):
            crows = slice(c * CHUNK, (c + 1) * CHUNK)
            orows = slice(g * group_rows + c * CHUNK, g * group_rows + (c + 1) * CHUNK)
            yc = y - head_sum(y) * inv_n
            yield
            yn = yc * lax.rsqrt(head_sum(yc * yc) * inv_n + LNX_EPS)
            yn = yn * prm(P_LNG) + prm(P_LNB)
            o_ref[orows, :] = ((yn + bonus[crows, :]) * d["gate"][crows, :]).astype(o_ref.dtype)
            yield

        tail = iter(())
        for c in group_chunks:
            x = [_mm(q["wr"][c, p], hs[p]) for p in pairs]
            next(tail, None)
            yield
            u = [x[p][:CHUNK, :] + q["ut"][c, p] for p in pairs]
            hs[:] = [d["gcol"][c][p] * (hs[p] + q["kv"][c, p]
                                        + jnp.where(bd_mask, _mm(d["bkt"][c][p][:, :CHUNK], u[p]), 0.0))
                     for p in pairs]
            y_p = []
            for p in pairs:
                uv = jnp.concatenate([u[p], blk(d["v"], c, p)], axis=0)
                y0 = _mm(q["aa"][c, p, 0][CHUNK:, :], uv)
                y1 = _mm(q["aa"][c, p, 1][CHUNK:, :], uv)
                y_p.append(x[p][CHUNK:, :] + jnp.where(lane_lo, y0, y1))
            y = jnp.concatenate(y_p, axis=1)
            next(tail, None)
            yield
            tail = finish(c, y)
        for _ in tail:
            yield

    hs = [state_ref[p] for p in pairs]
    n_groups = rows_t // group_rows
    pending = ()
    carry = [shift_ref[0:1, i * BLK:(i + 1) * BLK] for i in range(4)]
    d = prep(0, carry)
    for g in range(n_groups):
        q = {}
        stages = state_free_part(d, q)
        next(stages)
        d_next = prep(g + 1, carry) if g + 1 < n_groups else None
        for _ in itertools.zip_longest(stages, pending):
            pass
        pending = state_part(g, d, q, hs)
        d = d_next
    for _ in pending:
        pass
    for p in pairs:
        state_ref[p] = hs[p]
    for i in range(4):
        shift_ref[0:1, i * BLK:(i + 1) * BLK] = carry[i]


def _rwkv(h3, v_first, mu, prm, w2f, a2f, v2f, g2f, ones_blk):
    b, s, _ = h3.shape
    has_vres = v_first is not None
    first_col = RET_COLS // BLK
    t = min(RWKV_STEP, s)

    def col(j):
        return pl.BlockSpec((None, t, BLK), lambda bi, ci, j=j: (bi, ci, j))

    tok = pl.BlockSpec((None, t, RWKV_WIDTH), lambda bi, ci: (bi, ci, 0))
    tok_shape = jax.ShapeDtypeStruct((b, s, RWKV_WIDTH), F32)
    out_shape = jax.ShapeDtypeStruct((b, s, RWKV_WIDTH), BF16)
    in_specs = [col(first_col), col(first_col + 1), col(first_col + 2), col(first_col + 3)]
    args = [h3, h3, h3, h3]
    if has_vres:
        in_specs.append(tok)
        args.append(v_first)
    consts = (mu, prm, w2f, a2f, v2f, g2f, ones_blk)
    in_specs += [_const_spec(c.shape) for c in consts]
    args += list(consts)
    out = pl.pallas_call(
        functools.partial(_rwkv_kernel, has_vres=has_vres),
        grid=(b, s // t),
        in_specs=in_specs,
        out_specs=tok if has_vres else (tok, tok),
        out_shape=out_shape if has_vres else (out_shape, tok_shape),
        scratch_shapes=[
            pltpu.VMEM((CARRY_ROWS, 4 * BLK), F32),
            pltpu.VMEM((N_PAIRS, PAIR, PAIR), F32),
        ],
        compiler_params=_params("parallel", "arbitrary"),
        name="rwkv7",
    )(*args)
    return (out, v_first) if has_vres else out


def _merge_kernel(x_ref, ret_ref, rwkv_ref, ga_ref, gb_ref, wret_ref, wrwkv_ref, wout_ref,
                  lng_ref, lnb_ref, o_ref):
    ret_out = _mm(ret_ref[...], wret_ref[...])
    rwkv_out = _mm(rwkv_ref[...], wrwkv_ref[...])
    mixed = (_sigmoid(ga_ref[...].astype(F32)) * ret_out
             + _sigmoid(gb_ref[...].astype(F32)) * rwkv_out)
    y = DEEPNORM_ALPHA * x_ref[...] + _mm(mixed, wout_ref[...])
    o_ref[...] = _layer_norm(y, lng_ref[...], lnb_ref[...])


def _merge(x2d, ret2d, rwkv2d, h2d, w_ret, w_rwkv, w_out, ln_g, ln_b):
    m = x2d.shape[0]
    tm = min(ROW_TILE, m)
    gate_blk =(RET_COLS + RKV_COLS + LORA_COLS) // D_MODEL

    def rows(width, j=0):
        return pl.BlockSpec((tm, width), lambda i, j=j: (i, j))

    consts = (w_ret, w_rwkv, w_out, ln_g, ln_b)
    return pl.pallas_call(
        _merge_kernel,
        grid=(m // tm,),
        in_specs=[rows(D_MODEL), rows(RET_WIDTH), rows(RWKV_WIDTH), rows(D_MODEL, gate_blk),
                  rows(D_MODEL, gate_blk + 1)] + [_const_spec(c.shape) for c in consts],
        out_specs=rows(D_MODEL),
        out_shape=jax.ShapeDtypeStruct((m, D_MODEL), F32),
        compiler_params=_params("parallel"),
        name="merge_ln",
    )(x2d, ret2d, rwkv2d, h2d, h2d, *consts)


FF_CHUNK = D_FF
CONV_PAD = 8


def _ffn_kernel(x_ref, wg_ref, wv_ref, wd_ref, cw_ref, cb_ref, lng_ref, lnb_ref, o_ref,
                gate_ref):
    @pl.when(pl.program_id(1) == 0)
    def _():
        gate_ref[...] = jnp.zeros_like(gate_ref)

    t = x_ref.shape[0]
    x = x_ref[...]
    xb = x.astype(BF16)
    acc = DEEPNORM_ALPHA * x
    for c in range(D_FF // FF_CHUNK):
        cols = slice(c * FF_CHUNK, (c + 1) * FF_CHUNK)
        gate_ref[c, CONV_PAD:, :] = jnp.dot(xb, wg_ref[:, cols], preferred_element_type=F32)
        conv = (gate_ref[c, CONV_PAD - 2:CONV_PAD - 2 + t, :] * cw_ref[0:1, cols]
                + gate_ref[c, CONV_PAD - 1:CONV_PAD - 1 + t, :] * cw_ref[1:2, cols]
                + gate_ref[c, CONV_PAD:, :] * cw_ref[2:3, cols] + cb_ref[:, cols])
        gate_ref[c, CONV_PAD - 2:CONV_PAD, :] = gate_ref[c, CONV_PAD + t - 2:CONV_PAD + t, :]
        val = jnp.dot(xb, wv_ref[:, cols], preferred_element_type=F32)
        act = conv * _sigmoid(conv) * val
        acc = acc + _mm(act, wd_ref[cols, :])
    o_ref[...] = _layer_norm(acc, lng_ref[...], lnb_ref[...])


def _ffn(x3, w_gate, w_val, w_down, conv_w, conv_b, ln_g, ln_b):
    b, s, _ = x3.shape
    t = min(ROW_TILE, s)
    tok = pl.BlockSpec((None, t, D_MODEL), lambda bi, ti: (bi, ti, 0))
    consts = (w_gate, w_val, w_down, conv_w, conv_b, ln_g, ln_b)
    return pl.pallas_call(
        _ffn_kernel,
        grid=(b, s // t),
        in_specs=[tok] + [_const_spec(c.shape) for c in consts],
        out_specs=tok,
        out_shape=jax.ShapeDtypeStruct((b, s, D_MODEL), F32),
        scratch_shapes=[pltpu.VMEM((D_FF // FF_CHUNK, CONV_PAD + t, FF_CHUNK), F32)],
        compiler_params=_params("parallel", "arbitrary"),
        name="conv_mlp_ln",
    )(x3, *consts)


def _row(v):
    return v.reshape(1, -1).astype(F32)


def _pad_rows(w, start, total=LORA_COLS):
    return jnp.pad(w, ((start, total - start - w.shape[0]), (0, 0))).astype(BF16)


def kernel(x, w_in, mu_shift, w_vres_in, mu_vres, v0, v2, w0, w2, a0, a2, g2, k_k, k_a, r_k,
           lnx_g, lnx_b, w_ret_o, w_rwkv_o, w_out, ln1_g, ln1_b, w_up, conv_w, conv_b, w_down,
           ln2_g, ln2_b):
    b, s, d = x.shape
    assert d == D_MODEL and s % CHUNK == 0
    m = b * s
    cos_t, sin_t = _rope_tables(s)
    head_id = jnp.arange(RWKV_WIDTH) // RWKV_HEAD_DIM
    ones_blk = (head_id[:, None] == head_id[None, :]).astype(BF16)
    split = RET_COLS + RKV_COLS + LORA_USED
    lora_pad = LORA_COLS - LORA_USED - MV_LORA
    off_a, off_g, off_v = DECAY_LORA, DECAY_LORA + AAA_LORA, LORA_USED

    v_first = None
    for l in range(DEPTH):
        vres_w = w_vres_in[l - 1] if l > 0 else jnp.zeros((D_MODEL, MV_LORA), F32)
        vres_mu = mu_vres[l - 1] if l > 0 else jnp.zeros((MV_LORA,), F32)
        lora_start = RET_COLS + RKV_COLS
        w_groups = (
            w_in[l][:, :lora_start].astype(BF16),
            jnp.concatenate([w_in[l][:, lora_start:split], vres_w,
                             jnp.zeros((D_MODEL, lora_pad), F32)], axis=1).astype(BF16),
            w_in[l][:, split:].astype(BF16),
        )
        mu_lora = jnp.concatenate(
            [mu_shift[l][RKV_COLS:], vres_mu, jnp.zeros((lora_pad,), F32)])
        mu = jnp.concatenate([mu_shift[l][:RKV_COLS].reshape(3, RWKV_WIDTH), mu_lora[None]], 0)
        v0_l = v0[l - 1] if l > 0 else jnp.zeros((RWKV_WIDTH,), F32)
        prm = jnp.stack([w0[l], a0[l], v0_l, k_k[l], k_a[l], r_k[l].reshape(-1), lnx_g[l],
                         lnx_b[l]]).astype(F32)
        v2_l = v2[l - 1] if l > 0 else jnp.zeros((MV_LORA, RWKV_WIDTH), F32)
        w2f = _pad_rows(w2[l], 0, LORA_WIN_A)
        a2f = _pad_rows(a2[l], off_a, LORA_WIN_A)
        v2f = _pad_rows(v2_l, off_v - LORA_WIN_V[0], LORA_WIN_V[1] - LORA_WIN_V[0])
        g2f = _pad_rows(g2[l], off_g - LORA_WIN_G[0], LORA_WIN_G[1] - LORA_WIN_G[0])

        x2d = x.reshape(m, D_MODEL)
        h2d = _inproj(x2d, w_groups)
        h3 = h2d.reshape(b, s, N_IN)
        ret = _retention(h3, cos_t, sin_t)
        rwkv, v_first = _rwkv(h3, v_first, mu, prm, w2f, a2f, v2f, g2f, ones_blk)
        x1 = _merge(x2d, ret.reshape(m, RET_WIDTH), rwkv.reshape(m, RWKV_WIDTH), h2d,
                    w_ret_o[l].astype(BF16), w_rwkv_o[l].astype(BF16), w_out[l].astype(BF16),
                    _row(ln1_g[l]), _row(ln1_b[l]))
        x = _ffn(x1.reshape(b, s, D_MODEL), w_up[l][:, :D_FF].astype(BF16),
                 w_up[l][:, D_FF:].astype(BF16), w_down[l].astype(BF16), conv_w[l].astype(F32),
                 _row(conv_b[l]), _row(ln2_g[l]), _row(ln2_b[l]))
    return x
```

```python
import functools
import itertools
import math

import jax
import jax.numpy as jnp
from jax import lax
from jax.experimental import pallas as pl
from jax.experimental.pallas import tpu as pltpu

F32 = jnp.float32
BF16 = jnp.bfloat16

D_MODEL = 1024
DEPTH = 2
CHUNK = 64
RET_HEADS = 4
RET_HEAD_DIM = 128
RET_WIDTH = RET_HEADS * RET_HEAD_DIM
RWKV_HEADS = 8
RWKV_HEAD_DIM = 64
RWKV_WIDTH = RWKV_HEADS * RWKV_HEAD_DIM
DECAY_LORA = 64
AAA_LORA = 64
MV_LORA = 32
GATE_LORA = 160
D_FF = 2816
ROPE_BASE = 10000.0
LN_EPS = 1e-5
RET_NORM_EPS = 1e-6
LNX_EPS = 64e-5
DEEPNORM_ALPHA = (2 * DEPTH) ** 0.25

RET_COLS = 4 * RET_WIDTH
RKV_COLS = 3 * RWKV_WIDTH
LORA_COLS = 512
LORA_USED = DECAY_LORA + AAA_LORA + GATE_LORA
LORA_WIN_A = 128
LORA_WIN_G = (128, 384)
LORA_WIN_V = (256, 384)
N_IN = RET_COLS + RKV_COLS + LORA_COLS + 2 * D_MODEL
BLK = 512

VMEM_LIMIT = 56 * 1024 * 1024
ROW_TILE = 512

NT = ((1,), (1,))
TN = ((0,), (0,))
NN = ((1,), (0,))


def _mm(a, b, dims=NN):
    return lax.dot_general(a.astype(BF16), b.astype(BF16), (dims, ((), ())),
                           preferred_element_type=F32)


def _split2(x):
    hi = x.astype(BF16)
    lo = (x - hi.astype(F32)).astype(BF16)
    return hi, lo


def _sigmoid(x):
    return 1.0 / (1.0 + jnp.exp(-x))


def _layer_norm(y, g, b):
    mean = jnp.mean(y, axis=-1, keepdims=True)
    yc = y - mean
    var = jnp.mean(yc * yc, axis=-1, keepdims=True)
    return yc * lax.rsqrt(var + LN_EPS) * g + b


def _params(*sem):
    return pltpu.CompilerParams(dimension_semantics=sem, vmem_limit_bytes=VMEM_LIMIT)


def _const_spec(shape):
    nd = len(shape)
    return pl.BlockSpec(shape, lambda *_: (0,) * nd, pipeline_mode=pl.Buffered(1))


def _inproj_kernel(x_ref, *refs):
    w_refs, o_ref = refs[:-1], refs[-1]
    xb = x_ref[...].astype(BF16)
    base = 0
    for w_ref in w_refs:
        for j in range(0, w_ref.shape[1], BLK):
            o_ref[:, base + j:base + j + BLK] = jnp.dot(
                xb, w_ref[:, j:j + BLK], preferred_element_type=F32).astype(o_ref.dtype)
        base += w_ref.shape[1]


def _inproj(x2d, weights):
    m, k = x2d.shape
    n = sum(w.shape[1] for w in weights)
    tm = min(ROW_TILE, m)
    return pl.pallas_call(
        _inproj_kernel,
        grid=(m // tm,),
        in_specs=[pl.BlockSpec((tm, k), lambda i: (i, 0))] + [_const_spec(w.shape) for w in weights],
        out_specs=pl.BlockSpec((tm, n), lambda i: (i, 0)),
        out_shape=jax.ShapeDtypeStruct((m, n), BF16),
        compiler_params=_params("parallel"),
        name="inproj",
    )(x2d, *weights)


def _rope_kernel(cos_ref, sin_ref):
    rows = cos_ref.shape[0]
    base = pl.program_id(0) * rows
    pos = (lax.broadcasted_iota(jnp.int32, (rows, RET_HEAD_DIM), 0) + base).astype(F32)
    lane = lax.broadcasted_iota(jnp.int32, (rows, RET_HEAD_DIM), 1)
    half = RET_HEAD_DIM // 2
    fidx = jnp.where(lane < half, lane, lane - half).astype(F32)
    inv_freq = jnp.exp(fidx * (-math.log(ROPE_BASE) / half))
    ang = pos * inv_freq
    cos_ref[...] = jnp.cos(ang)
    s = jnp.sin(ang)
    sin_ref[...] = jnp.where(lane < half, -s, s)


def _rope_tables(seq):
    rows = min(256, seq)
    spec = pl.BlockSpec((rows, RET_HEAD_DIM), lambda i: (i, 0))
    shape = jax.ShapeDtypeStruct((seq, RET_HEAD_DIM), F32)
    return pl.pallas_call(
        _rope_kernel, grid=(seq // rows,), in_specs=[], out_specs=(spec, spec),
        out_shape=(shape, shape), compiler_params=_params("parallel"), name="rope_tables",
    )()


def _ret_kernel(q_ref, k_ref, v_ref, g_ref, cos_ref, sin_ref, o_ref, state_ref):
    @pl.when(pl.program_id(1) == 0)
    def _():
        state_ref[...] = jnp.zeros_like(state_ref)

    ii = lax.broadcasted_iota(jnp.int32, (CHUNK, CHUNK), 0)
    jj = lax.broadcasted_iota(jnp.int32, (CHUNK, CHUNK), 1)
    dist = jnp.abs(ii - jj).astype(F32)
    tpos = lax.broadcasted_iota(jnp.int32, (CHUNK, RET_HEAD_DIM), 0).astype(F32)
    scale = RET_HEAD_DIM ** -0.5
    chunks = range(q_ref.shape[0] // CHUNK)
    heads = range(RET_HEADS)
    units = [(c, h) for c in chunks for h in heads]
    log_g = [math.log(1.0 - 2.0 ** (-5.0 - h)) for h in heads]
    intra_decay = [jnp.exp(log_g[h] * dist) for h in heads]
    q_decay = [jnp.exp(log_g[h] * (tpos + 1.0)) for h in heads]
    k_decay = [jnp.exp(log_g[h] * (CHUNK - 1.0 - tpos)) for h in heads]

    q, k, v = {}, {}, {}
    for (c, h) in units:
        rows = slice(c * CHUNK, (c + 1) * CHUNK)
        cols = slice(h * RET_HEAD_DIM, (h + 1) * RET_HEAD_DIM)
        cosv = cos_ref[rows, :]
        sinv = sin_ref[rows, :]
        qq = q_ref[rows, cols].astype(F32)
        kk = k_ref[rows, cols].astype(F32)
        q[c, h] = qq * cosv + pltpu.roll(qq, RET_HEAD_DIM // 2, 1) * sinv
        k[c, h] = (kk * cosv + pltpu.roll(kk, RET_HEAD_DIM // 2, 1) * sinv) * scale
        v[c, h] = v_ref[rows, cols]
    scores = {u: _mm(q[u], k[u], NT) * intra_decay[u[1]] for u in units}
    kv = {u: _mm(k[u] * k_decay[u[1]], v[u], TN) for u in units}
    intra = {u: _mm(scores[u], v[u]) for u in units}
    r_in = {}
    for h in heads:
        state = state_ref[h]
        for c in chunks:
            r_in[c, h] = state
            state = state * math.exp(log_g[h] * CHUNK) + kv[c, h]
        state_ref[h] = state
    cross = {u: _mm(q[u], r_in[u]) * q_decay[u[1]] for u in units}
    for (c, h) in units:
        rows = slice(c * CHUNK, (c + 1) * CHUNK)
        cols = slice(h * RET_HEAD_DIM, (h + 1) * RET_HEAD_DIM)
        o = intra[c, h] + cross[c, h]
        o = o * lax.rsqrt(jnp.mean(o * o, axis=-1, keepdims=True) + RET_NORM_EPS)
        g = g_ref[rows, cols].astype(F32)
        o_ref[rows, cols] = (o * (g * _sigmoid(g))).astype(o_ref.dtype)


def _retention(h3, cos_t, sin_t):
    b, s, _ = h3.shape
    t = min(ROW_TILE, s)

    def col(j):
        return pl.BlockSpec((None, t, BLK), lambda bi, ti, j=j: (bi, ti, j))

    tab =pl.BlockSpec((t, RET_HEAD_DIM), lambda bi, ti: (ti, 0))
    return pl.pallas_call(
        _ret_kernel,
        grid=(b, s // t),
        in_specs=[col(0), col(1), col(2), col(3), tab, tab],
        out_specs=pl.BlockSpec((None, t, RET_WIDTH), lambda bi, ti: (bi, ti, 0)),
        out_shape=jax.ShapeDtypeStruct((b, s, RET_WIDTH), BF16),
        scratch_shapes=[pltpu.VMEM((RET_HEADS, RET_HEAD_DIM, RET_HEAD_DIM), F32)],
        compiler_params=_params("parallel", "arbitrary"),
        name="retention",
    )(h3, h3, h3, h3, cos_t, sin_t)


P_W0, P_A0, P_V0, P_KK, P_KA, P_RK, P_LNG, P_LNB = range(8)
RWKV_GROUP_CHUNKS = 4
RWKV_STEP_CHUNKS = 8
RWKV_STEP = RWKV_STEP_CHUNKS * CHUNK
CARRY_ROWS = 8
PAIR = 2 * RWKV_HEAD_DIM
N_PAIRS = RWKV_HEADS // 2


INV_BLK = 16


def _tri_inverse_stages(a, units, out):
    nb = CHUNK // INV_BLK
    r16 = lax.broadcasted_iota(jnp.int32, (INV_BLK, CHUNK), 0)
    l16 = lax.broadcasted_iota(jnp.int32, (INV_BLK, CHUNK), 1)
    lane_blk = l16 // INV_BLK
    cm = [lane_blk == i for i in range(nb)]
    eye_s = jnp.where(l16 - lane_blk * INV_BLK == r16, 1.0, 0.0)
    zero16 = jnp.zeros((INV_BLK, CHUNK), F32)
    half = CHUNK // 2
    left_half = lax.broadcasted_iota(jnp.int32, (half, CHUNK), 1) < half
    zero32 = jnp.zeros((half, CHUNK), F32)

    def rows(x, i):
        return x[i * INV_BLK:(i + 1) * INV_BLK, :]

    def pick(x, i):
        return jnp.where(cm[i], x, 0.0)

    def block_diag(s):
        return jnp.concatenate([pick(s, i) for i in range(nb)], axis=0)

    p = {u: pick(rows(a[u], 0), 0) + pick(rows(a[u], 1), 1) + pick(rows(a[u], 2), 2)
            + pick(rows(a[u], 3), 3) for u in units}
    t = {u: eye_s + p[u] for u in units}
    pbd = {u: block_diag(p[u]) for u in units}
    for _ in range(3):
        p = {u: _mm(p[u], pbd[u]) for u in units}
        yield
        pbd = {u: block_diag(p[u]) for u in units}
        t = {u: t[u] + _mm(t[u], pbd[u]) for u in units}
        yield
    x = {u: _mm(pick(rows(a[u], 1), 0) + pick(rows(a[u], 3), 2), block_diag(t[u])) for u in units}
    yield
    y = {u: _mm(pick(t[u], 1) + pick(t[u], 3),
                jnp.concatenate([zero16, pick(x[u], 0), zero16, pick(x[u], 2)], axis=0)) for u in units}
    yield
    t32 = {u: jnp.concatenate([pick(t[u], 0), pick(y[u], 0) + pick(t[u], 1),
                               pick(t[u], 2), pick(y[u], 2) + pick(t[u], 3)], axis=0) for u in units}
    x2 = {u: _mm(jnp.where(left_half, a[u][half:, :], 0.0), t32[u]) for u in units}
    yield
    z = {u: _mm(t32[u][half:, :], jnp.concatenate([zero32, x2[u]], axis=0)) for u in units}
    yield
    out.update({u: jnp.concatenate([t32[u][:half, :], z[u] + t32[u][half:, :]], axis=0)
                for u in units})


def _rwkv_kernel(*refs, has_vres):
    if has_vres:
        (hr_ref, hk_ref, hv_ref, hl_ref, vfirst_ref, mu_ref, prm_ref, w2_ref, a2_ref, v2_ref, g2_ref,
         ones_ref, o_ref, shift_ref, state_ref) = refs
    else:
        (hr_ref, hk_ref, hv_ref, hl_ref, mu_ref, prm_ref, w2_ref, a2_ref, v2_ref, g2_ref,
         ones_ref, o_ref, vout_ref, shift_ref, state_ref) = refs
    rows_t = hr_ref.shape[0]
    n_chunks = rows_t // CHUNK

    @pl.when(pl.program_id(1) == 0)
    def _():
        shift_ref[...] = jnp.zeros_like(shift_ref)
        state_ref[...] = jnp.zeros_like(state_ref)

    def prm(i):
        return prm_ref[i:i + 1, :]

    def head_sum(x):
        return _mm(x, ones_ref[...])

    group_rows = min(RWKV_GROUP_CHUNKS, n_chunks) * CHUNK
    group_chunks = range(group_rows // CHUNK)
    ti = lax.broadcasted_iota(jnp.int32, (group_rows, group_rows), 0)
    si = lax.broadcasted_iota(jnp.int32, (group_rows, group_rows), 1)
    lower = jnp.where(ti >= si, jnp.where((ti // CHUNK) == (si // CHUNK), 1.0, 0.0), 0.0).astype(BF16)
    pairs = range(N_PAIRS)

    shift_mat = jnp.where(ti == si + 1, 1.0, 0.0).astype(BF16)
    first8 = lax.broadcasted_iota(jnp.int32, (8, BLK), 0) == 0

    def prep(g, carry):
        rows = slice(g * group_rows, (g + 1) * group_rows)
        z = []
        for i, ref in enumerate((hr_ref, hk_ref, hv_ref, hl_ref)):
            cur_b = ref[rows, :]
            cur = cur_b.astype(F32)
            prev = lax.dot_general(shift_mat, cur_b, (NN, ((), ())), preferred_element_type=F32)
            prev = jnp.concatenate([jnp.where(first8, carry[i], prev[:8, :]), prev[8:, :]], axis=0)
            z.append(cur + mu_ref[i:i + 1, :] * (prev - cur))
            carry[i] = cur[group_rows - 1:group_rows, :]
        r, k, v, lora = z
        lora_wa = lora[:, :LORA_WIN_A]
        dw = prm(P_W0) + _mm(jnp.tanh(lora_wa), w2_ref[...])
        log_decay = -math.exp(-0.5) * _sigmoid(dw)
        a = _sigmoid(prm(P_A0) + _mm(lora_wa, a2_ref[...]))
        if has_vres:
            v_mix = _mm(lora[:, LORA_WIN_V[0]:LORA_WIN_V[1]], v2_ref[...])
            v = v + (vfirst_ref[rows, :] - v) * _sigmoid(prm(P_V0) + v_mix)
        else:
            vout_ref[rows, :] = v
        gate = _mm(_sigmoid(lora[:, LORA_WIN_G[0]:LORA_WIN_G[1]]), g2_ref[...])
        kk = k * prm(P_KK)
        kk = kk * lax.rsqrt(jnp.maximum(head_sum(kk * kk), 1e-24))
        k = k * (1.0 + (a - 1.0) * prm(P_KA))
        cum = None
        for part in _split2(log_decay):
            t = lax.dot_general(lower, part, (NN, ((), ())), preferred_element_type=F32)
            cum = t if cum is None else cum + t
        e_in = jnp.exp(cum)
        e_neg = jnp.exp(-cum)
        b_t = kk * a * e_neg
        k_t = k * e_neg
        bkt, gcol = [], []
        for c in group_chunks:
            crows = slice(c * CHUNK, (c + 1) * CHUNK)
            bk_t = jnp.concatenate([b_t[crows, :], k_t[crows, :]], axis=0).T
            g_t = jnp.broadcast_to(e_in[(c + 1) * CHUNK - 1:(c + 1) * CHUNK, :], (PAIR, RWKV_WIDTH)).T
            bkt.append([bk_t[p * PAIR:(p + 1) * PAIR, :] for p in pairs])
            gcol.append([g_t[p * PAIR:(p + 1) * PAIR, :] for p in pairs])
        return dict(r=r, k=k, v=v, gate=gate, a_t=-kk * jnp.exp(cum - log_decay), r_t=r * e_in,
                    bkt=bkt, gcol=gcol)

    row = lax.broadcasted_iota(jnp.int32, (PAIR, PAIR), 0)
    lane = lax.broadcasted_iota(jnp.int32, (PAIR, PAIR), 1)
    hd = RWKV_HEAD_DIM
    row_hi = jnp.where(row < hd, 0, 1)
    lane_hi = jnp.where(lane < hd, 0, 1)
    tri_mask = (row - hd * row_hi + row_hi) > (lane - hd * lane_hi)
    bd_mask = row_hi == lane_hi
    lane_lo = lax.broadcasted_iota(jnp.int32, (CHUNK, PAIR), 1) < hd
    zeros = jnp.zeros((CHUNK, PAIR), F32)
    units = [(c, p, e) for c in group_chunks for p in pairs for e in range(2)]
    cps = [(c, p) for c in group_chunks for p in pairs]

    def blk(x, c, p):
        return x[c * CHUNK:(c + 1) * CHUNK, p * PAIR:(p + 1) * PAIR]

    def state_free_part(d, out):
        aa = {}
        for (c, p) in cps:
            bkt = d["bkt"][c][p]
            rhs = jnp.concatenate([jnp.concatenate([bkt[:hd, :], zeros], axis=0),
                                   jnp.concatenate([zeros, bkt[hd:, :]], axis=0)], axis=1)
            res = _mm(jnp.concatenate([blk(d["a_t"], c, p), blk(d["r_t"], c, p)], axis=0), rhs)
            for e in range(2):
                aa[c, p, e] = jnp.where(tri_mask, res[:, e * PAIR:(e + 1) * PAIR], 0.0)
        yield
        tinv = {}
        yield from _tri_inverse_stages({u: aa[u][:CHUNK, :CHUNK] for u in units}, units, tinv)
        akv, kv = {}, {}
        for (c, p) in cps:
            zv = jnp.concatenate([zeros, blk(d["v"], c, p)], axis=0)
            res = _mm(jnp.concatenate([aa[c, p, 0][:CHUNK, :], aa[c, p, 1][:CHUNK, :],
                                       d["bkt"][c][p]], axis=0), zv)
            akv[c, p, 0], akv[c, p, 1] = res[:CHUNK, :], res[CHUNK:2 * CHUNK, :]
            kv[c, p] = jnp.where(bd_mask, res[2 * CHUNK:, :], 0.0)
        yield
        tw = {(c, p, e): _mm(tinv[c, p, e], jnp.concatenate([blk(d["a_t"], c, p), akv[c, p, e]], axis=1))
              for (c, p, e) in units}
        yield
        out["aa"] = aa
        out["ut"] = {(c, p): jnp.where(lane_lo, tw[c, p, 0][:, PAIR:], tw[c, p, 1][:, PAIR:])
                     for (c, p) in cps}
        out["wr"] = {(c, p): jnp.concatenate(
            [jnp.where(lane_lo, tw[c, p, 0][:, :PAIR], tw[c, p, 1][:, :PAIR]), blk(d["r_t"], c, p)],
            axis=0) for (c, p) in cps}
        out["kv"] = kv
        yield

    def state_part(g, d, q, hs):
        inv_n = 1.0 / RWKV_HEAD_DIM
        bonus = head_sum(d["r"] * d["k"] * prm(P_RK)) * d["v"]

        ys = []
        for c in group_chunks:
            x = [_mm(q["wr"][c, p], hs[p]) for p in pairs]
            yield
            u = [x[p][:CHUNK, :] + q["ut"][c, p] for p in pairs]
            hs[:] = [d["gcol"][c][p] * (hs[p] + q["kv"][c, p]
                                        + jnp.where(bd_mask, _mm(d["bkt"][c][p][:, :CHUNK], u[p]), 0.0))
                     for p in pairs]
            y_p = []
            for p in pairs:
                uv = jnp.concatenate([u[p], blk(d["v"], c, p)], axis=0)
                y01 = _mm(jnp.concatenate([q["aa"][c, p, 0][CHUNK:, :], q["aa"][c, p, 1][CHUNK:, :]],
                                          axis=0), uv)
                y_p.append(x[p][CHUNK:, :] + jnp.where(lane_lo, y01[:CHUNK, :], y01[CHUNK:, :]))
            ys.append(jnp.concatenate(y_p, axis=1))
            yield
        y = jnp.concatenate(ys, axis=0)
        orows = slice(g * group_rows, (g + 1) * group_rows)
        yc = y - head_sum(y) * inv_n
        yield
        yn = yc * lax.rsqrt(head_sum(yc * yc) * inv_n + LNX_EPS)
        yn = yn * prm(P_LNG) + prm(P_LNB)
        o_ref[orows, :] = ((yn + bonus) * d["gate"]).astype(o_ref.dtype)
        yield

    hs = [state_ref[p] for p in pairs]
    n_groups = rows_t // group_rows
    pending = ()
    carry = [shift_ref[0:1, i * BLK:(i + 1) * BLK] for i in range(4)]
    d = prep(0, carry)
    for g in range(n_groups):
        q = {}
        stages = state_free_part(d, q)
        next(stages)
        d_next = prep(g + 1, carry) if g + 1 < n_groups else None
        for _ in itertools.zip_longest(stages, pending):
            pass
        pending = state_part(g, d, q, hs)
        d = d_next
    for _ in pending:
        pass
    for p in pairs:
        state_ref[p] = hs[p]
    for i in range(4):
        shift_ref[0:1, i * BLK:(i + 1) * BLK] = carry[i]


def _rwkv(h3, v_first, mu, prm, w2f, a2f, v2f, g2f, ones_blk):
    b, s, _ = h3.shape
    has_vres = v_first is not None
    first_col = RET_COLS // BLK
    t = min(RWKV_STEP, s)

    def col(j):
        return pl.BlockSpec((None, t, BLK), lambda bi, ci, j=j: (bi, ci, j))

    tok = pl.BlockSpec((None, t, RWKV_WIDTH), lambda bi, ci: (bi, ci, 0))
    tok_shape = jax.ShapeDtypeStruct((b, s, RWKV_WIDTH), F32)
    out_shape = jax.ShapeDtypeStruct((b, s, RWKV_WIDTH), BF16)
    in_specs = [col(first_col), col(first_col + 1), col(first_col + 2), col(first_col + 3)]
    args = [h3, h3, h3, h3]
    if has_vres:
        in_specs.append(tok)
        args.append(v_first)
    consts = (mu, prm, w2f, a2f, v2f, g2f, ones_blk)
    in_specs += [_const_spec(c.shape) for c in consts]
    args += list(consts)
    out = pl.pallas_call(
        functools.partial(_rwkv_kernel, has_vres=has_vres),
        grid=(b, s // t),
        in_specs=in_specs,
        out_specs=tok if has_vres else (tok, tok),
        out_shape=out_shape if has_vres else (out_shape, tok_shape),
        scratch_shapes=[
            pltpu.VMEM((CARRY_ROWS, 4 * BLK), F32),
            pltpu.VMEM((N_PAIRS, PAIR, PAIR), F32),
        ],
        compiler_params=_params("parallel", "arbitrary"),
        name="rwkv7",
    )(*args)
    return (out, v_first) if has_vres else out


def _merge_kernel(x_ref, ret_ref, rwkv_ref, ga_ref, gb_ref, wret_ref, wrwkv_ref, wout_ref,
                  lng_ref, lnb_ref, o_ref):
    ret_out = _mm(ret_ref[...], wret_ref[...])
    rwkv_out = _mm(rwkv_ref[...], wrwkv_ref[...])
    mixed = (_sigmoid(ga_ref[...].astype(F32)) * ret_out
             + _sigmoid(gb_ref[...].astype(F32)) * rwkv_out)
    y = DEEPNORM_ALPHA * x_ref[...] + _mm(mixed, wout_ref[...])
    o_ref[...] = _layer_norm(y, lng_ref[...], lnb_ref[...])


def _merge(x2d, ret2d, rwkv2d, h2d, w_ret, w_rwkv, w_out, ln_g, ln_b):
    m = x2d.shape[0]
    tm = min(ROW_TILE, m)
    gate_blk =(RET_COLS + RKV_COLS + LORA_COLS) // D_MODEL

    def rows(width, j=0):
        return pl.BlockSpec((tm, width), lambda i, j=j: (i, j))

    consts = (w_ret, w_rwkv, w_out, ln_g, ln_b)
    return pl.pallas_call(
        _merge_kernel,
        grid=(m // tm,),
        in_specs=[rows(D_MODEL), rows(RET_WIDTH), rows(RWKV_WIDTH), rows(D_MODEL, gate_blk),
                  rows(D_MODEL, gate_blk + 1)] + [_const_spec(c.shape) for c in consts],
        out_specs=rows(D_MODEL),
        out_shape=jax.ShapeDtypeStruct((m, D_MODEL), F32),
        compiler_params=_params("parallel"),
        name="merge_ln",
    )(x2d, ret2d, rwkv2d, h2d, h2d, *consts)


FF_CHUNK = D_FF
CONV_PAD = 8


def _ffn_kernel(x_ref, wg_ref, wv_ref, wd_ref, cw_ref, cb_ref, lng_ref, lnb_ref, o_ref,
                gate_ref):
    @pl.when(pl.program_id(1) == 0)
    def _():
        gate_ref[...] = jnp.zeros_like(gate_ref)

    t = x_ref.shape[0]
    x = x_ref[...]
    xb = x.astype(BF16)
    acc = DEEPNORM_ALPHA * x
    for c in range(D_FF // FF_CHUNK):
        cols = slice(c * FF_CHUNK, (c + 1) * FF_CHUNK)
        gate_ref[c, CONV_PAD:, :] = jnp.dot(xb, wg_ref[:, cols], preferred_element_type=F32)
        conv = (gate_ref[c, CONV_PAD - 2:CONV_PAD - 2 + t, :] * cw_ref[0:1, cols]
                + gate_ref[c, CONV_PAD - 1:CONV_PAD - 1 + t, :] * cw_ref[1:2, cols]
                + gate_ref[c, CONV_PAD:, :] * cw_ref[2:3, cols] + cb_ref[:, cols])
        gate_ref[c, CONV_PAD - 2:CONV_PAD, :] = gate_ref[c, CONV_PAD + t - 2:CONV_PAD + t, :]
        val = jnp.dot(xb, wv_ref[:, cols], preferred_element_type=F32)
        act = conv * _sigmoid(conv) * val
        acc = acc + _mm(act, wd_ref[cols, :])
    o_ref[...] = _layer_norm(acc, lng_ref[...], lnb_ref[...])


def _ffn(x3, w_gate, w_val, w_down, conv_w, conv_b, ln_g, ln_b):
    b, s, _ = x3.shape
    t = min(ROW_TILE, s)
    tok = pl.BlockSpec((None, t, D_MODEL), lambda bi, ti: (bi, ti, 0))
    consts = (w_gate, w_val, w_down, conv_w, conv_b, ln_g, ln_b)
    return pl.pallas_call(
        _ffn_kernel,
        grid=(b, s // t),
        in_specs=[tok] + [_const_spec(c.shape) for c in consts],
        out_specs=tok,
        out_shape=jax.ShapeDtypeStruct((b, s, D_MODEL), F32),
        scratch_shapes=[pltpu.VMEM((D_FF // FF_CHUNK, CONV_PAD + t, FF_CHUNK), F32)],
        compiler_params=_params("parallel", "arbitrary"),
        name="conv_mlp_ln",
    )(x3, *consts)


def _row(v):
    return v.reshape(1, -1).astype(F32)


def _pad_rows(w, start, total=LORA_COLS):
    return jnp.pad(w, ((start, total - start - w.shape[0]), (0, 0))).astype(BF16)


def kernel(x, w_in, mu_shift, w_vres_in, mu_vres, v0, v2, w0, w2, a0, a2, g2, k_k, k_a, r_k,
           lnx_g, lnx_b, w_ret_o, w_rwkv_o, w_out, ln1_g, ln1_b, w_up, conv_w, conv_b, w_down,
           ln2_g, ln2_b):
    b, s, d = x.shape
    assert d == D_MODEL and s % CHUNK == 0
    m = b * s
    cos_t, sin_t = _rope_tables(s)
    head_id = jnp.arange(RWKV_WIDTH) // RWKV_HEAD_DIM
    ones_blk = (head_id[:, None] == head_id[None, :]).astype(BF16)
    split = RET_COLS + RKV_COLS + LORA_USED
    lora_pad = LORA_COLS - LORA_USED - MV_LORA
    off_a, off_g, off_v = DECAY_LORA, DECAY_LORA + AAA_LORA, LORA_USED

    v_first = None
    for l in range(DEPTH):
        vres_w = w_vres_in[l - 1] if l > 0 else jnp.zeros((D_MODEL, MV_LORA), F32)
        vres_mu = mu_vres[l - 1] if l > 0 else jnp.zeros((MV_LORA,), F32)
        lora_start = RET_COLS + RKV_COLS
        w_groups = (
            w_in[l][:, :lora_start].astype(BF16),
            jnp.concatenate([w_in[l][:, lora_start:split], vres_w,
                             jnp.zeros((D_MODEL, lora_pad), F32)], axis=1).astype(BF16),
            w_in[l][:, split:].astype(BF16),
        )
        mu_lora = jnp.concatenate(
            [mu_shift[l][RKV_COLS:], vres_mu, jnp.zeros((lora_pad,), F32)])
        mu = jnp.concatenate([mu_shift[l][:RKV_COLS].reshape(3, RWKV_WIDTH), mu_lora[None]], 0)
        v0_l = v0[l - 1] if l > 0 else jnp.zeros((RWKV_WIDTH,), F32)
        prm = jnp.stack([w0[l], a0[l], v0_l, k_k[l], k_a[l], r_k[l].reshape(-1), lnx_g[l],
                         lnx_b[l]]).astype(F32)
        v2_l = v2[l - 1] if l > 0 else jnp.zeros((MV_LORA, RWKV_WIDTH), F32)
        w2f = _pad_rows(w2[l], 0, LORA_WIN_A)
        a2f = _pad_rows(a2[l], off_a, LORA_WIN_A)
        v2f = _pad_rows(v2_l, off_v - LORA_WIN_V[0], LORA_WIN_V[1] - LORA_WIN_V[0])
        g2f = _pad_rows(g2[l], off_g - LORA_WIN_G[0], LORA_WIN_G[1] - LORA_WIN_G[0])

        x2d = x.reshape(m, D_MODEL)
        h2d = _inproj(x2d, w_groups)
        h3 = h2d.reshape(b, s, N_IN)
        ret = _retention(h3, cos_t, sin_t)
        rwkv, v_first = _rwkv(h3, v_first, mu, prm, w2f, a2f, v2f, g2f, ones_blk)
        x1 = _merge(x2d, ret.reshape(m, RET_WIDTH), rwkv.reshape(m, RWKV_WIDTH), h2d,
                    w_ret_o[l].astype(BF16), w_rwkv_o[l].astype(BF16), w_out[l].astype(BF16),
                    _row(ln1_g[l]), _row(ln1_b[l]))
        x = _ffn(x1.reshape(b, s, D_MODEL), w_up[l][:, :D_FF].astype(BF16),
                 w_up[l][:, D_FF:].astype(BF16), w_down[l].astype(BF16), conv_w[l].astype(F32),
                 _row(conv_b[l]), _row(ln2_g[l]), _row(ln2_b[l]))
    return x
```
